```python
import jax
import jax.numpy as jnp
from jax import lax
import numpy as np

D_MODEL = 2048
BATCH = 4
SEQ = 2048
DEPTH = 4
DEC_BATCH = 128
DEC_SEQ = 1
PAST_LEN = 16384
PAGE_SIZE = 128

N_BRANCH = 3
BRANCH_W = D_MODEL // 2
RW_HEAD = 64
RW_HEADS = BRANCH_W // RW_HEAD
RW_DECAY_LORA = 64
RW_A_LORA = 64
RW_G_LORA = 128
RW_SPLITS = (BRANCH_W, BRANCH_W, BRANCH_W, RW_DECAY_LORA, RW_A_LORA, RW_G_LORA)
RW_COLS = sum(RW_SPLITS)
RW_GN_EPS = 64e-5
GLA_HEADS = 4
GLA_DK = BRANCH_W // 2
GLA_DV = BRANCH_W
GLA_HK = GLA_DK // GLA_HEADS
GLA_HV = GLA_DV // GLA_HEADS
GLA_GATE_LORA = 16
GLA_TAU = 16.0
GLA_CHUNK = 64
GLA_SPLITS = (GLA_DK, GLA_DK, GLA_DV, GLA_GATE_LORA, BRANCH_W)
GLA_COLS = sum(GLA_SPLITS)
CM_W = BRANCH_W
CM_CHUNK = 128
CM_GROUPS = 8
CM_GW = CM_W // CM_GROUPS
CM_COLS = 2 * CM_W
GATE_COLS = N_BRANCH * D_MODEL
IN_SPLITS = (RW_COLS, GLA_COLS, CM_COLS, GATE_COLS)
IN_COLS = sum(IN_SPLITS)
PEER_HEADS = 8
PEER_NKEYS = 128
PEER_N = PEER_NKEYS * PEER_NKEYS
PEER_DQ = 256
PEER_TOPK = 16
PEER_BLOCK = 128
RMS_EPS = 1e-6
LN_EPS = 1e-5

kernel_name = 'hybrid_rwkv7_gla_gmlp_peer_step'


def _split(t, sizes):
    return jnp.split(t, np.cumsum(sizes)[:-1].tolist(), axis=-1)


def rms_norm(x, g):
    xf = x.astype(jnp.float32)
    y = xf * lax.rsqrt(jnp.mean(xf * xf, -1, keepdims=True) + RMS_EPS)
    return (y * g.astype(jnp.float32)).astype(x.dtype)


def layer_norm(x, w, b, eps):
    xf = x.astype(jnp.float32)
    xc = xf - jnp.mean(xf, -1, keepdims=True)
    var = jnp.mean(xc * xc, -1, keepdims=True)
    return xc * lax.rsqrt(var + eps) * w.astype(jnp.float32) + b.astype(jnp.float32)


def rwkv7_scan(r, log_w, k, v, kk, a, S0):
    def step(S, inp):
        r_t, w_t, k_t, v_t, kk_t, a_t = inp
        sa = jnp.einsum('bhvk,bhk->bhv', S, -kk_t)
        S = (S * jnp.exp(w_t)[:, :, None, :]
             + sa[..., None] * (kk_t * a_t)[:, :, None, :]
             + v_t[..., None] * k_t[:, :, None, :])
        return S, jnp.einsum('bhvk,bhk->bhv', S, r_t)
    xs = tuple(jnp.moveaxis(t, 1, 0) for t in (r, log_w, k, v, kk, a))
    S, ys = lax.scan(step, S0, xs)
    return jnp.moveaxis(ys, 0, 1), S


def rwkv7_branch(p, shift_prev, S0, mu, w0, w2, a0, a2, g2, k_k, k_a, r_k, ln_w, ln_b):
    B, T, _ = p.shape
    f32 = jnp.float32
    prev = jnp.concatenate([shift_prev[:, None, :].astype(p.dtype), p[:, :-1]], axis=1)
    xs = p + (prev - p) * mu
    r, k, v, xw, xa, xg = _split(xs, RW_SPLITS)
    w_log = -jax.nn.softplus(-(w0 + jnp.tanh(xw) @ w2).astype(f32)) - 0.5
    log_w = -jnp.exp(w_log)
    a = jax.nn.sigmoid((a0 + xa @ a2).astype(f32))
    g = (jax.nn.sigmoid(xg) @ g2).astype(f32)
    hs = lambda t: t.astype(f32).reshape(B, T, RW_HEADS, RW_HEAD)
    r_h, k_h, v_h, a_h, w_h = hs(r), hs(k), hs(v), hs(a), hs(log_w)
    kk = k_h * k_k.astype(f32).reshape(RW_HEADS, RW_HEAD)
    kk = kk * lax.rsqrt(jnp.maximum(jnp.sum(kk * kk, -1, keepdims=True), 1e-24))
    k_h = k_h * (1.0 + (a_h - 1.0) * k_a.astype(f32).reshape(RW_HEADS, RW_HEAD))
    y, S = rwkv7_scan(r_h, w_h, k_h, v_h, kk, a_h, S0.astype(f32))
    y = layer_norm(y, ln_w.reshape(RW_HEADS, RW_HEAD), ln_b.reshape(RW_HEADS, RW_HEAD), RW_GN_EPS)
    bonus = jnp.sum(r_h * k_h * r_k.astype(f32), -1, keepdims=True) * v_h
    out = (y + bonus).reshape(B, T, BRANCH_W) * g
    return out.astype(p.dtype), p[:, -1], S


def gla_chunked(q, k, v, log_a, S0):
    B, T, H, dk = q.shape
    dv = v.shape[-1]
    L = GLA_CHUNK
    pad = -(-T // L) * L - T
    n = (T + pad) // L
    padr = lambda t: jnp.pad(t, ((0, 0), (0, pad), (0, 0), (0, 0))).reshape(B, n, L, H, t.shape[-1])
    q, k, v, log_a = padr(q), padr(k), padr(v), padr(log_a)
    b = jnp.cumsum(log_a, axis=2)
    qe = q * jnp.exp(b)
    ke = k * jnp.exp(-b)
    kd = k * jnp.exp(b[:, :, -1:] - b)
    A = jnp.einsum('bnihd,bnjhd->bnhij', qe, ke)
    A = jnp.where(jnp.tril(jnp.ones((L, L), bool)), A, 0.0)
    o_intra = jnp.einsum('bnhij,bnjhv->bnihv', A, v)
    dS = jnp.einsum('bnjhd,bnjhv->bnhdv', kd, v)
    decay = jnp.exp(b[:, :, -1])

    def step(S, inp):
        qe_c, dS_c, dec_c = inp
        o = jnp.einsum('bihd,bhdv->bihv', qe_c, S)
        return S * dec_c[..., None] + dS_c, o

    S, o_inter = lax.scan(step, S0, (jnp.moveaxis(qe, 1, 0), jnp.moveaxis(dS, 1, 0), jnp.moveaxis(decay, 1, 0)))
    o = o_intra + jnp.moveaxis(o_inter, 0, 1)
    return o.reshape(B, n * L, H, dv)[:, :T], S


def gla_branch(p, S0, gw2, gb, ln_w, ln_b):
    B, T, _ = p.shape
    f32 = jnp.float32
    q, k, v, xg, r = _split(p, GLA_SPLITS)
    q = q.astype(f32).reshape(B, T, GLA_HEADS, GLA_HK) * (GLA_HK ** -0.5)
    k = k.astype(f32).reshape(B, T, GLA_HEADS, GLA_HK)
    v = v.astype(f32).reshape(B, T, GLA_HEADS, GLA_HV)
    log_a = (jax.nn.log_sigmoid((xg @ gw2 + gb).astype(f32)) / GLA_TAU).reshape(B, T, GLA_HEADS, GLA_HK)
    o, S = gla_chunked(q, k, v, log_a, S0.astype(f32))
    o = layer_norm(o, ln_w.reshape(GLA_HEADS, GLA_HV), ln_b.reshape(GLA_HEADS, GLA_HV), LN_EPS).reshape(B, T, BRANCH_W)
    return (o * jax.nn.silu(r.astype(f32))).astype(p.dtype), S


def chunk_mlp_branch(p, ln_w, ln_b, ws, bias):
    B, T, _ = p.shape
    z = jax.nn.gelu(p, approximate=False)
    u, v = _split(z, (CM_W, CM_W))
    v = layer_norm(v, ln_w, ln_b, LN_EPS).astype(p.dtype)
    pad = -(-T // CM_CHUNK) * CM_CHUNK - T
    n = (T + pad) // CM_CHUNK
    vc = jnp.pad(v, ((0, 0), (0, pad), (0, 0))).reshape(B, n, CM_CHUNK, CM_GROUPS, CM_GW)
    ws_c = jnp.where(jnp.tril(jnp.ones((CM_CHUNK, CM_CHUNK), bool)), ws, 0.0)
    mixed = jnp.einsum('gij,bnjgc->bnigc', ws_c, vc) + bias.T[:, :, None]
    mixed = mixed.reshape(B, n * CM_CHUNK, CM_W)[:, :T]
    open_start = ((T - 1) // CM_CHUNK) * CM_CHUNK
    return (u * mixed).astype(p.dtype), v[:, open_start:]


def peer_ffn(x, wq, keys, eu, ev):
    B, T, D = x.shape
    f32 = jnp.float32
    n = B * T
    xt = x.reshape(n, D)
    q = (xt @ wq).astype(f32).reshape(n, PEER_HEADS, 2, PEER_DQ // 2)
    s = jnp.einsum('nhpd,hpkd->nhpk', q, keys.astype(f32))
    s1, i1 = lax.top_k(s[:, :, 0], PEER_TOPK)
    s2, i2 = lax.top_k(s[:, :, 1], PEER_TOPK)
    cand = (s1[..., :, None] + s2[..., None, :]).reshape(n, PEER_HEADS, PEER_TOPK * PEER_TOPK)
    cidx = (i1[..., :, None] * PEER_NKEYS + i2[..., None, :]).reshape(n, PEER_HEADS, PEER_TOPK * PEER_TOPK)
    top, sel = lax.top_k(cand, PEER_TOPK)
    idx = jnp.take_along_axis(cidx, sel, axis=-1).reshape(n, PEER_HEADS * PEER_TOPK)
    gate = jax.nn.softmax(top, axis=-1).reshape(n, PEER_HEADS * PEER_TOPK)
    npad = -(-n // PEER_BLOCK) * PEER_BLOCK - n
    nb = (n + npad) // PEER_BLOCK
    xb = jnp.pad(xt, ((0, npad), (0, 0))).reshape(nb, PEER_BLOCK, D)
    ib = jnp.pad(idx, ((0, npad), (0, 0))).reshape(nb, PEER_BLOCK, -1)
    gb = jnp.pad(gate, ((0, npad), (0, 0))).reshape(nb, PEER_BLOCK, -1)

    def block(args):
        xc, ic, gc = args
        h = jnp.einsum('tkd,td->tk', eu[ic], xc).astype(f32)
        w = (gc * jax.nn.gelu(h, approximate=False)).astype(xc.dtype)
        return jnp.einsum('tk,tkd->td', w, ev[ic])

    y = lax.map(block, (xb, ib, gb))
    return y.reshape(nb * PEER_BLOCK, D)[:n].reshape(B, T, D)


def run_trunk(x, shift0, srw0, sgla0, P):
    B, T, D = x.shape
    shifts, srws, sglas, cmvs = [], [], [], []
    for l in range(DEPTH):
        h = rms_norm(x, P['norm1'][l])
        p = h @ P['w_in'][l]
        p_rw, p_gla, p_cm, p_gate = _split(p, IN_SPLITS)
        o_rw, sh, srw = rwkv7_branch(p_rw, shift0[l], srw0[l], P['rw_mu'][l], P['rw_w0'][l], P['rw_w2'][l],
                                     P['rw_a0'][l], P['rw_a2'][l], P['rw_g2'][l], P['rw_kk'][l], P['rw_ka'][l],
                                     P['rw_rk'][l], P['rw_lnw'][l], P['rw_lnb'][l])
        o_gla, sgla = gla_branch(p_gla, sgla0[l], P['gla_gw2'][l], P['gla_gb'][l], P['gla_lnw'][l], P['gla_lnb'][l])
        o_cm, cmv = chunk_mlp_branch(p_cm, P['cm_lnw'][l], P['cm_lnb'][l], P['cm_ws'][l], P['cm_b'][l])
        branches = jnp.stack([o_rw, o_gla, o_cm], axis=2)
        proj = jnp.einsum('btgc,gcd->btgd', branches, P['w_branch'][l])
        gates = jax.nn.sigmoid(p_gate.reshape(B, T, N_BRANCH, D))
        x = x + (jnp.sum(gates * proj, axis=2) @ P['w_out'][l]).astype(x.dtype)
        hf = rms_norm(x, P['norm2'][l])
        x = x + peer_ffn(hf, P['peer_wq'][l], P['peer_keys'][l], P['peer_u'][l], P['peer_v'][l]).astype(x.dtype)
        shifts.append(sh)
        srws.append(srw)
        sglas.append(sgla)
        cmvs.append(cmv)
    return rms_norm(x, P['norm_f']), jnp.stack(shifts), jnp.stack(srws), jnp.stack(sglas), jnp.stack(cmvs)


def setup_inputs(seed: int = 0) -> dict:
    key = jax.random.key(seed)
    ks = iter(jax.random.split(key, 48))
    f32 = jnp.float32
    nrm = lambda shape, s: jax.random.normal(next(ks), shape, f32) * s
    uni = lambda shape, lo, hi: jax.random.uniform(next(ks), shape, f32, lo, hi)
    L = DEPTH
    return {
        'x_prompt': nrm((BATCH, SEQ, D_MODEL), 1.0),
        'x_sample': nrm((DEC_BATCH, DEC_SEQ, D_MODEL), 1.0),
        'state_rw_shift': nrm((L, DEC_BATCH, RW_COLS), 1.0),
        'state_rwkv': nrm((L, DEC_BATCH, RW_HEADS, RW_HEAD, RW_HEAD), 0.3),
        'state_gla': nrm((L, DEC_BATCH, GLA_HEADS, GLA_HK, GLA_HV), 0.3),
        'norm1': 1.0 + nrm((L, D_MODEL), 0.02),
        'w_in': nrm((L, D_MODEL, IN_COLS), D_MODEL ** -0.5),
        'rw_mu': uni((L, RW_COLS), 0.0, 1.0),
        'rw_w0': uni((L, BRANCH_W), -6.0, 1.0),
        'rw_w2': nrm((L, RW_DECAY_LORA, BRANCH_W), 0.1),
        'rw_a0': nrm((L, BRANCH_W), 0.1),
        'rw_a2': nrm((L, RW_A_LORA, BRANCH_W), 0.1),
        'rw_g2': nrm((L, RW_G_LORA, BRANCH_W), RW_G_LORA ** -0.5),
        'rw_kk': 0.85 + nrm((L, BRANCH_W), 0.02),
        'rw_ka': 1.0 + nrm((L, BRANCH_W), 0.02),
        'rw_rk': nrm((L, RW_HEADS, RW_HEAD), 0.1),
        'rw_lnw': 1.0 + nrm((L, BRANCH_W), 0.02),
        'rw_lnb': nrm((L, BRANCH_W), 0.02),
        'gla_gw2': nrm((L, GLA_GATE_LORA, GLA_DK), GLA_GATE_LORA ** -0.5),
        'gla_gb': nrm((L, GLA_DK), 0.1),
        'gla_lnw': 1.0 + nrm((L, BRANCH_W), 0.02),
        'gla_lnb': nrm((L, BRANCH_W), 0.02),
        'cm_lnw': 1.0 + nrm((L, CM_W), 0.02),
        'cm_lnb': nrm((L, CM_W), 0.02),
        'cm_ws': nrm((L, CM_GROUPS, CM_CHUNK, CM_CHUNK), CM_CHUNK ** -0.5),
        'cm_b': 1.0 + nrm((L, CM_GROUPS, CM_CHUNK), 0.02),
        'w_branch': nrm((L, N_BRANCH, BRANCH_W, D_MODEL), BRANCH_W ** -0.5),
        'w_out': nrm((L, D_MODEL, D_MODEL), D_MODEL ** -0.5),
        'norm2': 1.0 + nrm((L, D_MODEL), 0.02),
        'peer_wq': nrm((L, D_MODEL, PEER_HEADS * PEER_DQ), D_MODEL ** -0.5),
        'peer_keys': nrm((L, PEER_HEADS, 2, PEER_NKEYS, PEER_DQ // 2), (PEER_DQ // 2) ** -0.5),
        'peer_u': nrm((L, PEER_N, D_MODEL), D_MODEL ** -0.5),
        'peer_v': nrm((L, PEER_N, D_MODEL), (PEER_HEADS * PEER_TOPK) ** -0.5),
        'norm_f': 1.0 + nrm((D_MODEL,), 0.02),
    }


def reference(x_prompt, x_sample, state_rw_shift, state_rwkv, state_gla, norm1, w_in, rw_mu, rw_w0, rw_w2,
              rw_a0, rw_a2, rw_g2, rw_kk, rw_ka, rw_rk, rw_lnw, rw_lnb, gla_gw2, gla_gb, gla_lnw, gla_lnb,
              cm_lnw, cm_lnb, cm_ws, cm_b, w_branch, w_out, norm2, peer_wq, peer_keys, peer_u, peer_v, norm_f):
    P = {'norm1': norm1, 'w_in': w_in, 'rw_mu': rw_mu, 'rw_w0': rw_w0, 'rw_w2': rw_w2, 'rw_a0': rw_a0,
         'rw_a2': rw_a2, 'rw_g2': rw_g2, 'rw_kk': rw_kk, 'rw_ka': rw_ka, 'rw_rk': rw_rk, 'rw_lnw': rw_lnw,
         'rw_lnb': rw_lnb, 'gla_gw2': gla_gw2, 'gla_gb': gla_gb, 'gla_lnw': gla_lnw, 'gla_lnb': gla_lnb,
         'cm_lnw': cm_lnw, 'cm_lnb': cm_lnb, 'cm_ws': cm_ws, 'cm_b': cm_b, 'w_branch': w_branch,
         'w_out': w_out, 'norm2': norm2, 'peer_wq': peer_wq, 'peer_keys': peer_keys, 'peer_u': peer_u,
         'peer_v': peer_v, 'norm_f': norm_f}
    B = x_prompt.shape[0]
    z_shift = jnp.zeros((DEPTH, B, RW_COLS), x_prompt.dtype)
    z_rwkv = jnp.zeros((DEPTH, B, RW_HEADS, RW_HEAD, RW_HEAD), jnp.float32)
    z_gla = jnp.zeros((DEPTH, B, GLA_HEADS, GLA_HK, GLA_HV), jnp.float32)
    y_prompt, p_shift, p_rwkv, p_gla, p_cmv = run_trunk(x_prompt, z_shift, z_rwkv, z_gla, P)
    y_sample, s_shift, s_rwkv, s_gla, s_cmv = run_trunk(x_sample, state_rw_shift, state_rwkv, state_gla, P)
    return (y_prompt, y_sample, p_shift, p_rwkv, p_gla, p_cmv, s_shift, s_rwkv, s_gla, s_cmv)
```

```python
import functools
import math

import jax
import jax.numpy as jnp
from jax import lax
from jax.experimental import pallas as pl
from jax.experimental.pallas import tpu as pltpu

F32 = jnp.float32
BF16 = jnp.bfloat16

D_MODEL = 2048
DEPTH = 4
BRANCH_W = 1024
RW_HEAD = 64
RW_HEADS = 16
RW_PAIRS = RW_HEADS // 2
RW_COLS = 3328
RW_GN_EPS = 64e-5
RW_CHUNK = 64
GLA_HEADS = 4
GLA_HK = 128
GLA_HV = 256
GLA_DK = 512
GLA_GATE_LORA = 16
GLA_TAU = 16.0
GLA_CHUNK = 64
CM_W = 1024
CM_CHUNK = 128
CM_GROUPS = 8
CM_GW = 128
PEER_HEADS = 8
PEER_NKEYS = 128
PEER_N = PEER_NKEYS * PEER_NKEYS
PEER_DH = 128
PEER_TOPK = 16
PEER_ITILES = 16
RMS_EPS = 1e-6
LN_EPS = 1e-5

LANES = 128
SUBLANES = 8
VMEM_LIMIT_BYTES = 56 * 1024 * 1024

NEG_INF = float("-inf")

_NN = (((1,), (0,)), ((), ()))
_NT = (((1,), (1,)), ((), ()))
_TN = (((0,), (0,)), ((), ()))


def _params(*sem):
    return pltpu.CompilerParams(dimension_semantics=sem, vmem_limit_bytes=VMEM_LIMIT_BYTES)


def _bdot(a, b, dims=_NN):
    return lax.dot_general(a.astype(BF16), b.astype(BF16), dims, preferred_element_type=F32)


def _hdot(a, b, dims=_NN):
    return lax.dot_general(a.astype(F32), b.astype(F32), dims, preferred_element_type=F32,
                           precision=lax.Precision.HIGHEST)


def _gelu(x):
    return 0.5 * x * (1.0 + lax.erf(x * (1.0 / math.sqrt(2.0))))


def _sigmoid(x):
    return 1.0 / (1.0 + jnp.exp(-x))


def _softplus(x):
    return jnp.maximum(x, 0.0) + jnp.log(1.0 + jnp.exp(-jnp.abs(x)))


def _iota(shape, dim):
    return lax.broadcasted_iota(jnp.int32, shape, dim)


def _rmsnorm_kernel(x_ref, g_ref, o_ref):
    x = x_ref[...]
    ms = jnp.mean(x * x, axis=-1, keepdims=True)
    o_ref[...] = (x * lax.rsqrt(ms + RMS_EPS) * g_ref[...]).astype(o_ref.dtype)


def rmsnorm(x, g, out_dtype, tm):
    n, d = x.shape
    assert n % tm == 0
    return pl.pallas_call(
        _rmsnorm_kernel,
        grid=(n // tm,),
        in_specs=[pl.BlockSpec((tm, d), lambda i: (i, 0)), pl.BlockSpec((1, d), lambda i: (0, 0))],
        out_specs=pl.BlockSpec((tm, d), lambda i: (i, 0)),
        out_shape=jax.ShapeDtypeStruct((n, d), out_dtype),
        compiler_params=_params("parallel"),
        name="rmsnorm",
    )(x, g.reshape(1, d))


def _matmul_kernel(a_ref, w_ref, o_ref):
    o_ref[...] = jnp.dot(a_ref[...], w_ref[...], preferred_element_type=F32)


def _matmul_res_kernel(a_ref, w_ref, r_ref, o_ref):
    o_ref[...] = r_ref[...] + jnp.dot(a_ref[...], w_ref[...], preferred_element_type=F32)


def matmul(a, w, tm, tn, res=None):
    n, k = a.shape
    m = w.shape[1]
    assert n % tm == 0 and m % tn == 0
    in_specs = [pl.BlockSpec((tm, k), lambda j, i: (i, 0)), pl.BlockSpec((k, tn), lambda j, i: (0, j))]
    args = [a, w]
    body = _matmul_kernel
    if res is not None:
        in_specs.append(pl.BlockSpec((tm, tn), lambda j, i: (i, j)))
        args.append(res)
        body = _matmul_res_kernel
    return pl.pallas_call(
        body,
        grid=(m // tn, n // tm),
        in_specs=in_specs,
        out_specs=pl.BlockSpec((tm, tn), lambda j, i: (i, j)),
        out_shape=jax.ShapeDtypeStruct((n, m), F32),
        compiler_params=_params("parallel", "parallel"),
        name="matmul",
    )(*args)


def _pair_stack(x, lane_a):
    return jnp.concatenate([jnp.where(lane_a, x, 0.0), jnp.where(lane_a, 0.0, x)], axis=0)


def _rwkv_kernel(p_ref, prev_ref, s0_ref, mu_ref, w0_ref, wa2_ref, a0_ref, g2_ref, kk_ref, ka_ref, rk_ref,
                 lnw_ref, lnb_ref, ones_ref, o_ref, shift_ref, sout_ref, s_scr, prev_scr, y_scr, *, C, t_valid):
    c = pl.program_id(1)
    n_chunks = pl.num_programs(1)
    H = RW_HEAD

    @pl.when(c == 0)
    def _init():
        prev_scr[...] = jnp.broadcast_to(prev_ref[0], prev_scr.shape)
        s_scr[...] = jnp.zeros_like(s_scr)
        for j in range(RW_PAIRS):
            s_scr[j, 0:H, 0:H] = s0_ref[0, 2 * j]
            s_scr[j, H:2 * H, H:2 * H] = s0_ref[0, 2 * j + 1]

    p = p_ref[0]
    row = _iota((C, 1), 0)
    prev = jnp.where(row == 0, prev_scr[0:1, :], pltpu.roll(p, 1, axis=0))
    last_row = (t_valid - 1) % C
    prev_scr[...] = jnp.broadcast_to(p[last_row:last_row + 1, :], prev_scr.shape)
    xs = p + (prev - p) * mu_ref[...]
    r = xs[:, 0:1024]
    k = xs[:, 1024:2048]
    v = xs[:, 2048:3072]
    z = xs[:, 3072:3200]
    xg = xs[:, 3200:3328]
    lane128 = _iota((1, LANES), 1)
    tz = jnp.where(lane128 < RW_HEAD, jnp.tanh(z), z)
    wa = _bdot(tz, wa2_ref[...])
    w_lin = w0_ref[...] + wa[:, 0:1024]
    log_w = -jnp.exp(-_softplus(-w_lin) - 0.5)
    a = _sigmoid(a0_ref[...] + wa[:, 1024:2048])
    g = _bdot(_sigmoid(xg), g2_ref[...])
    ones_bd = ones_ref[...]

    def headsum(x):
        return jnp.concatenate(
            [_hdot(x[:, LANES * j:LANES * (j + 1)], ones_bd) for j in range(RW_PAIRS)], axis=1)

    kk = k * kk_ref[...]
    kk = kk * lax.rsqrt(jnp.maximum(headsum(kk * kk), 1e-24))
    k_hat = k * (1.0 + (a - 1.0) * ka_ref[...])
    b = kk * a
    if t_valid % C != 0:
        valid = (c * C + row) < t_valid
        log_w = jnp.where(valid, log_w, 0.0)
        kk = jnp.where(valid, kk, 0.0)
        b = jnp.where(valid, b, 0.0)
        k_hat = jnp.where(valid, k_hat, 0.0)
    tri = (_iota((C, C), 0) >= _iota((C, C), 1)).astype(F32)
    cum = _hdot(tri, log_w)
    cum_last = cum[C - 1:C, :]
    kap_t = kk * jnp.exp(cum - log_w)
    r_t = r * jnp.exp(cum)
    inv = jnp.exp(-cum)
    b_t = b * inv
    k_t = k_hat * inv
    to_end = jnp.exp(cum_last - cum)
    b_e = b * to_end
    k_e = k_hat * to_end
    p_end = jnp.exp(cum_last)

    lane_a = lane128 < RW_HEAD
    tt = _iota((C, 2 * C), 0)
    ss = _iota((C, 2 * C), 1) % C
    strict = tt > ss
    lower = tt >= ss
    eye_cat = (tt == ss).astype(F32)
    blockdiag = (_iota((LANES, LANES), 0) < RW_HEAD) == (_iota((LANES, LANES), 1) < RW_HEAD)

    for j in range(RW_PAIRS):
        sl = slice(LANES * j, LANES * (j + 1))
        s_pair = s_scr[j]
        kr = jnp.concatenate([kap_t[:, sl], r_t[:, sl]], axis=0)
        xs_s = _hdot(kr, s_pair, _NT)
        g1 = _hdot(kr, _pair_stack(b_t[:, sl], lane_a), _NT)
        g2 = _hdot(kr, _pair_stack(k_t[:, sl], lane_a), _NT)
        a1 = jnp.where(strict, g1[0:C], 0.0)
        a3 = jnp.where(lower, g1[C:2 * C], 0.0)
        a2 = jnp.where(strict, g2[0:C], 0.0)
        a4 = jnp.where(lower, g2[C:2 * C], 0.0)
        tm = eye_cat - jnp.where((tt // 2 == ss // 2), a1, 0.0)
        m = 2
        while m < C:
            sel = (tt // (2 * m) == ss // (2 * m)) & ((tt // m) % 2 == 1) & ((ss // m) % 2 == 0)
            e = jnp.where(sel, a1, 0.0)
            inner = _hdot(e, _pair_stack_cat(tm, C))
            tm = tm - _hdot(tm, _pair_stack_cat(inner, C))
            m *= 2
        v_bd = _pair_stack(v[:, sl], lane_a)
        av = _hdot(jnp.concatenate([a2, a4], axis=0), v_bd)
        u = -_hdot(tm, _pair_stack(xs_s[0:C] + av[0:C], lane_a))
        y = xs_s[C:2 * C] + av[C:2 * C] + _hdot(a3, _pair_stack(u, lane_a))
        ds = _hdot(jnp.concatenate([u, v[:, sl]], axis=0),
                   jnp.concatenate([b_e[:, sl], k_e[:, sl]], axis=0), _TN)
        s_scr[j] = s_pair * p_end[:, sl] + jnp.where(blockdiag, ds, 0.0)
        y_scr[:, sl] = y

    y = y_scr[...]
    mean = headsum(y) * (1.0 / RW_HEAD)
    yc = y - mean
    var = headsum(yc * yc) * (1.0 / RW_HEAD)
    yn = yc * lax.rsqrt(var + RW_GN_EPS) * lnw_ref[...] + lnb_ref[...]
    bonus = headsum(r * k_hat_raw(k, a, ka_ref[...]) * rk_ref[...]) * v
    o_ref[0] = (yn + bonus) * g

    @pl.when(c == n_chunks - 1)
    def _fin():
        shift_ref[0] = p[last_row:last_row + 1, :]
        for j in range(RW_PAIRS):
            sout_ref[0, 2 * j] = s_scr[j, 0:H, 0:H]
            sout_ref[0, 2 * j + 1] = s_scr[j, H:2 * H, H:2 * H]


def k_hat_raw(k, a, ka):
    return k * (1.0 + (a - 1.0) * ka)


def _pair_stack_cat(x, C):
    first = _iota((1, 2 * C), 1) < C
    return jnp.concatenate([jnp.where(first, x, 0.0), jnp.where(first, 0.0, x)], axis=0)


def rwkv_branch(p3, shift_prev, s0, prm, C, t_valid):
    B, T, _ = p3.shape
    assert T % C == 0
    n_chunks = T // C
    row = lambda w: pl.BlockSpec((1, w), lambda b, c: (0, 0))
    full = lambda a: pl.BlockSpec(a.shape, lambda b, c: (0,) * a.ndim)
    kern = functools.partial(_rwkv_kernel, C=C, t_valid=t_valid)
    return pl.pallas_call(
        kern,
        grid=(B, n_chunks),
        in_specs=[
            pl.BlockSpec((1, C, RW_COLS), lambda b, c: (b, c, 0)),
            pl.BlockSpec((1, 1, RW_COLS), lambda b, c: (b, 0, 0)),
            pl.BlockSpec((1, RW_HEADS, RW_HEAD, RW_HEAD), lambda b, c: (b, 0, 0, 0)),
            row(RW_COLS), row(1024), full(prm["wa2"]), row(1024), full(prm["g2"]),
            row(1024), row(1024), row(1024), row(1024), row(1024), full(prm["ones_bd"]),
        ],
        out_specs=[
            pl.BlockSpec((1, C, BRANCH_W), lambda b, c: (b, c, 0)),
            pl.BlockSpec((1, 1, RW_COLS), lambda b, c: (b, 0, 0)),
            pl.BlockSpec((1, RW_HEADS, RW_HEAD, RW_HEAD), lambda b, c: (b, 0, 0, 0)),
        ],
        out_shape=[
            jax.ShapeDtypeStruct((B, T, BRANCH_W), F32),
            jax.ShapeDtypeStruct((B, 1, RW_COLS), F32),
            jax.ShapeDtypeStruct((B, RW_HEADS, RW_HEAD, RW_HEAD), F32),
        ],
        scratch_shapes=[
            pltpu.VMEM((RW_PAIRS, LANES, LANES), F32),
            pltpu.VMEM((SUBLANES, RW_COLS), F32),
            pltpu.VMEM((C, BRANCH_W), F32),
        ],
        compiler_params=_params("parallel", "arbitrary"),
        name="rwkv7",
    )(p3, shift_prev, s0, prm["mu"], prm["w0"], prm["wa2"], prm["a0"], prm["g2"], prm["kk"], prm["ka"],
      prm["rk"], prm["lnw"], prm["lnb"], prm["ones_bd"])


def _layernorm_lanes(x, w, b, eps):
    mean = jnp.mean(x, axis=-1, keepdims=True)
    xc = x - mean
    var = jnp.mean(xc * xc, axis=-1, keepdims=True)
    return xc * lax.rsqrt(var + eps) * w + b


def _gla_kernel(pm_ref, pxg_ref, s0_ref, gw2_ref, gb_ref, lnw_ref, lnb_ref, o_ref, sout_ref, s_scr, *, C, t_valid):
    c = pl.program_id(1)
    n_chunks = pl.num_programs(1)

    @pl.when(c == 0)
    def _init():
        s_scr[...] = s0_ref[0]

    pm = pm_ref[0]
    q = pm[:, 0:GLA_DK] * (GLA_HK ** -0.5)
    k = pm[:, GLA_DK:2 * GLA_DK]
    v = pm[:, 1024:2048]
    r = pm[:, 2048:3072]
    zl = _bdot(pxg_ref[0], gw2_ref[...]) + gb_ref[...]
    log_a = (jnp.minimum(zl, 0.0) - jnp.log(1.0 + jnp.exp(-jnp.abs(zl)))) * (1.0 / GLA_TAU)
    row = _iota((C, 1), 0)
    if t_valid % C != 0:
        log_a = jnp.where((c * C + row) < t_valid, log_a, 0.0)
    tri = _iota((C, C), 0) >= _iota((C, C), 1)
    cum = _hdot(tri.astype(F32), log_a)
    cum_last = cum[C - 1:C, :]
    qe = q * jnp.exp(cum)
    ke = k * jnp.exp(-cum)
    kd = k * jnp.exp(cum_last - cum)
    decay = jnp.exp(cum_last)
    silu_r = r * _sigmoid(r)
    for h in range(GLA_HEADS):
        kl = slice(GLA_HK * h, GLA_HK * (h + 1))
        vl = slice(GLA_HV * h, GLA_HV * (h + 1))
        s_h = s_scr[h]
        a_mat = jnp.where(tri, _bdot(qe[:, kl], ke[:, kl], _NT), 0.0)
        o = _bdot(a_mat, v[:, vl]) + _bdot(qe[:, kl], s_h)
        ds = _bdot(kd[:, kl], v[:, vl], _TN)
        dcol = jnp.transpose(jnp.broadcast_to(decay[:, kl], (GLA_HK, GLA_HK)))
        s_scr[h] = s_h * jnp.concatenate([dcol, dcol], axis=1) + ds
        o = _layernorm_lanes(o, lnw_ref[:, vl], lnb_ref[:, vl], LN_EPS)
        o_ref[0, :, vl] = o * silu_r[:, vl]

    @pl.when(c == n_chunks - 1)
    def _fin():
        sout_ref[0] = s_scr[...]


def gla_branch(pm3, pxg3, s0, prm, C, t_valid):
    B, T, _ = pm3.shape
    assert T % C == 0
    row = lambda w: pl.BlockSpec((1, w), lambda b, c: (0, 0))
    kern = functools.partial(_gla_kernel, C=C, t_valid=t_valid)
    return pl.pallas_call(
        kern,
        grid=(B, T // C),
        in_specs=[
            pl.BlockSpec((1, C, 3072), lambda b, c: (b, c, 0)),
            pl.BlockSpec((1, C, LANES), lambda b, c: (b, c, 0)),
            pl.BlockSpec((1, GLA_HEADS, GLA_HK, GLA_HV), lambda b, c: (b, 0, 0, 0)),
            pl.BlockSpec((LANES, GLA_DK), lambda b, c: (0, 0)),
            row(GLA_DK), row(BRANCH_W), row(BRANCH_W),
        ],
        out_specs=[
            pl.BlockSpec((1, C, BRANCH_W), lambda b, c: (b, c, 0)),
            pl.BlockSpec((1, GLA_HEADS, GLA_HK, GLA_HV), lambda b, c: (b, 0, 0, 0)),
        ],
        out_shape=[
            jax.ShapeDtypeStruct((B, T, BRANCH_W), F32),
            jax.ShapeDtypeStruct((B, GLA_HEADS, GLA_HK, GLA_HV), F32),
        ],
        scratch_shapes=[pltpu.VMEM((GLA_HEADS, GLA_HK, GLA_HV), F32)],
        compiler_params=_params("parallel", "arbitrary"),
        name="gla",
    )(pm3, pxg3, s0, prm["gw2"], prm["gb"], prm["lnw"], prm["lnb"])


def _cmlp_kernel(p_ref, lnw_ref, lnb_ref, ws_ref, bt_ref, o_ref, v_ref, *, C):
    c = pl.program_id(1)
    n_chunks = pl.num_programs(1)
    z = _gelu(p_ref[0])
    u = z[:, 0:CM_W]
    vn = _layernorm_lanes(z[:, CM_W:2 * CM_W], lnw_ref[...], lnb_ref[...], LN_EPS)
    tri = _iota((C, C), 0) >= _iota((C, C), 1)
    for g in range(CM_GROUPS):
        gl = slice(CM_GW * g, CM_GW * (g + 1))
        ws_c = jnp.where(tri, ws_ref[g], 0.0)
        mixed = _bdot(ws_c, vn[:, gl]) + bt_ref[:, g:g + 1]
        o_ref[0, :, gl] = u[:, gl] * mixed

    @pl.when(c == n_chunks - 1)
    def _fin():
        v_ref[0] = vn


def cmlp_branch(p3, prm, C):
    B, T, _ = p3.shape
    assert T % C == 0
    row = lambda w: pl.BlockSpec((1, w), lambda b, c: (0, 0))
    return pl.pallas_call(
        functools.partial(_cmlp_kernel, C=C),
        grid=(B, T // C),
        in_specs=[
            pl.BlockSpec((1, C, 2 * CM_W), lambda b, c: (b, c, 0)),
            row(CM_W), row(CM_W),
            pl.BlockSpec((CM_GROUPS, C, C), lambda b, c: (0, 0, 0)),
            pl.BlockSpec((C, CM_GROUPS), lambda b, c: (0, 0)),
        ],
        out_specs=[
            pl.BlockSpec((1, C, CM_W), lambda b, c: (b, c, 0)),
            pl.BlockSpec((1, C, CM_W), lambda b, c: (b, 0, 0)),
        ],
        out_shape=[
            jax.ShapeDtypeStruct((B, T, CM_W), F32),
            jax.ShapeDtypeStruct((B, C, CM_W), F32),
        ],
        compiler_params=_params("parallel", "arbitrary"),
        name="cmlp",
    )(p3, prm["lnw"], prm["lnb"], prm["ws"], prm["bt"])


def _mix_kernel(orw_ref, ogl_ref, ocm_ref, g0_ref, g1_ref, g2_ref, wb_ref, o_ref):
    acc = _sigmoid(g0_ref[...]) * _bdot(orw_ref[...], wb_ref[0])
    acc += _sigmoid(g1_ref[...]) * _bdot(ogl_ref[...], wb_ref[1])
    acc += _sigmoid(g2_ref[...]) * _bdot(ocm_ref[...], wb_ref[2])
    o_ref[...] = acc.astype(o_ref.dtype)


def branch_mix(o_rw, o_gla, o_cm, p_gate, wb, tm, tn):
    n = o_rw.shape[0]
    nj = D_MODEL // tn
    ospec = pl.BlockSpec((tm, BRANCH_W), lambda i, j: (i, 0))
    gspec = lambda g: pl.BlockSpec((tm, tn), lambda i, j: (i, g * nj + j))
    return pl.pallas_call(
        _mix_kernel,
        grid=(n // tm, nj),
        in_specs=[ospec, ospec, ospec, gspec(0), gspec(1), gspec(2),
                  pl.BlockSpec((3, BRANCH_W, tn), lambda i, j: (0, 0, j))],
        out_specs=pl.BlockSpec((tm, tn), lambda i, j: (i, j)),
        out_shape=jax.ShapeDtypeStruct((n, D_MODEL), BF16),
        compiler_params=_params("parallel", "parallel"),
        name="branch_mix",
    )(o_rw, o_gla, o_cm, p_gate, p_gate, p_gate, wb)


def _top_rows(s, n):
    out = []
    for _ in range(n):
        m = jnp.max(s, axis=0, keepdims=True)
        out.append(m)
        s = jnp.where(s == m, NEG_INF, s)
    return out


def _router_kernel(q_ref, keys_ref, beta_ref, thr_ref, c_ref):
    tn = q_ref.shape[0]
    n_top = PEER_TOPK + 1
    for h in range(PEER_HEADS):
        s1 = _hdot(keys_ref[2 * h], q_ref[:, LANES * (2 * h):LANES * (2 * h + 1)], _NT)
        s2 = _hdot(keys_ref[2 * h + 1], q_ref[:, LANES * (2 * h + 1):LANES * (2 * h + 2)], _NT)
        a = _top_rows(s1, n_top)
        b = _top_rows(s2, n_top)
        pad = jnp.full((3 * SUBLANES - n_top, tn), NEG_INF, F32)
        b_all = jnp.concatenate(b + [pad], axis=0)
        cand = jnp.concatenate([a[0] + b_all] + [a[x] + b_all[0:SUBLANES] for x in range(1, n_top)], axis=0)
        best = _top_rows(cand, n_top)
        top = best[0]
        zsum = jnp.zeros_like(top)
        for t in range(PEER_TOPK):
            zsum += jnp.exp(best[t] - top)
        tau = 0.5 * (best[PEER_TOPK - 1] + best[PEER_TOPK])
        rows = slice(SUBLANES * h, SUBLANES * (h + 1))
        beta_ref[h] = s2 - b[0]
        thr_ref[:, rows, :] = ((tau - top) - (s1 - a[0])).reshape(PEER_ITILES, SUBLANES, tn)
        c_ref[:, rows, :] = (jnp.exp(s1 - a[0]) / zsum).reshape(PEER_ITILES, SUBLANES, tn)


def peer_router(q, keys, tn):
    n = q.shape[0]
    spec = pl.BlockSpec((PEER_HEADS, PEER_NKEYS, tn), lambda i: (0, 0, i))
    shp = jax.ShapeDtypeStruct((PEER_HEADS, PEER_NKEYS, n), F32)
    spec_i = pl.BlockSpec((PEER_ITILES, PEER_HEADS * SUBLANES, tn), lambda i: (0, 0, i))
    shp_i = jax.ShapeDtypeStruct((PEER_ITILES, PEER_HEADS * SUBLANES, n), F32)
    return pl.pallas_call(
        _router_kernel,
        grid=(n // tn,),
        in_specs=[pl.BlockSpec((tn, D_MODEL), lambda i: (i, 0)),
                  pl.BlockSpec((2 * PEER_HEADS, PEER_NKEYS, PEER_DH), lambda i: (0, 0, 0))],
        out_specs=[spec, spec_i, spec_i],
        out_shape=[shp, shp_i, shp_i],
        compiler_params=_params("parallel"),
        name="peer_router",
    )(q, keys)


def _peer_kernel(xn_ref, u_ref, v_ref, beta_ref, thr_ref, c_ref, res_ref, o_ref, eb_scr, *, ti):
    e = pl.program_id(1)

    @pl.when(e == 0)
    def _init():
        o_ref[...] = res_ref[...]
        eb_scr[...] = jnp.exp(beta_ref[...])

    ht = lax.dot_general(u_ref[...], xn_ref[...], _NT, preferred_element_type=F32)
    sub = (e % (SUBLANES // ti)) * ti
    gates = []
    for ii in range(ti):
        acc = None
        for h in range(PEER_HEADS):
            r = pl.ds(SUBLANES * h + sub + ii, 1)
            hit = beta_ref[h] >= thr_ref[0, r, :]
            term = jnp.where(hit, eb_scr[h] * c_ref[0, r, :], 0.0)
            acc = term if acc is None else acc + term
        gates.append(acc)
    wd = (jnp.concatenate(gates, axis=0) * _gelu(ht)).astype(BF16)
    o_ref[...] += lax.dot_general(wd, v_ref[...], _TN, preferred_element_type=F32)


def peer_experts(xn, u, v, beta, thr, cc, res, tm, ti):
    n = xn.shape[0]
    te = ti * PEER_NKEYS
    ispec = pl.BlockSpec((1, PEER_HEADS * SUBLANES, tm), lambda i, e: (e // (SUBLANES // ti), 0, i))
    return pl.pallas_call(
        functools.partial(_peer_kernel, ti=ti),
        grid=(n // tm, PEER_NKEYS // ti),
        in_specs=[
            pl.BlockSpec((tm, D_MODEL), lambda i, e: (i, 0)),
            pl.BlockSpec((te, D_MODEL), lambda i, e: (e, 0)),
            pl.BlockSpec((te, D_MODEL), lambda i, e: (e, 0)),
            pl.BlockSpec((PEER_HEADS, PEER_NKEYS, tm), lambda i, e: (0, 0, i)),
            ispec, ispec,
            pl.BlockSpec((tm, D_MODEL), lambda i, e: (i, 0)),
        ],
        out_specs=pl.BlockSpec((tm, D_MODEL), lambda i, e: (i, 0)),
        out_shape=jax.ShapeDtypeStruct((n, D_MODEL), F32),
        scratch_shapes=[pltpu.VMEM((PEER_HEADS, PEER_NKEYS, tm), F32)],
        compiler_params=_params("parallel", "arbitrary"),
        name="peer_experts",
    )(xn, u, v, beta, thr, cc, res)


TOKEN_TILE = 640
W_IN_COLS = dict(rw=(0, 3328), gla_qkv=(3328, 5376), gla_xg=(5376, 5392), gla_r=(5392, 6416),
                 cm=(6416, 8464), gate=(8464, 14608))


def _proj(hn, w, tn):
    return matmul(hn, w.astype(BF16), TOKEN_TILE, tn)


def _pad_steps(a, steps):
    return jnp.pad(a, ((0, 0), (0, steps - a.shape[1]), (0, 0)))


def kernel(x_prompt, x_sample, state_rw_shift, state_rwkv, state_gla, norm1, w_in, rw_mu, rw_w0, rw_w2, rw_a0, rw_a2, rw_g2, rw_kk, rw_ka, rw_rk, rw_lnw, rw_lnb, gla_gw2, gla_gb, gla_lnw, gla_lnb, cm_lnw, cm_lnb, cm_ws, cm_b, w_branch, w_out, norm2, peer_wq, peer_keys, peer_u, peer_v, norm_f):
    bp, tp, d = x_prompt.shape
    bs = x_sample.shape[0]
    n_p = bp * tp
    x = jnp.concatenate([x_prompt.reshape(n_p, d), x_sample.reshape(bs, d)], axis=0)
    head_of_lane = jnp.arange(LANES) // RW_HEAD
    ones_bd = (head_of_lane[:, None] == head_of_lane[None, :]).astype(F32)
    row = lambda a: a.reshape(1, -1)
    dec_steps = SUBLANES
    cm_chunk_p = min(CM_CHUNK, tp)
    outs = {k: [] for k in ("p_sh", "p_rw", "p_gla", "p_cm", "s_sh", "s_rw", "s_gla", "s_cm")}
    for l in range(DEPTH):
        wl = w_in[l]
        cols = lambda name: wl[:, W_IN_COLS[name][0]:W_IN_COLS[name][1]]
        hn = rmsnorm(x, norm1[l], BF16, TOKEN_TILE)
        p_rw = _proj(hn, cols("rw"), 1664)
        p_gl = _proj(hn, jnp.concatenate([cols("gla_qkv"), cols("gla_r")], axis=1), 1536)
        p_xg = _proj(hn, jnp.pad(cols("gla_xg"), ((0, 0), (0, LANES - GLA_GATE_LORA))), LANES)
        p_cm = _proj(hn, cols("cm"), 1024)
        p_gt = _proj(hn, cols("gate"), 1536)

        split = lambda p: (p[:n_p].reshape(bp, tp, -1), _pad_steps(p[n_p:].reshape(bs, 1, -1), dec_steps))
        join = lambda a, b: jnp.concatenate([a.reshape(n_p, -1), b[:, 0, :]], axis=0)

        wa2 = jnp.zeros((LANES, 2 * BRANCH_W), F32)
        wa2 = wa2.at[0:RW_HEAD, 0:BRANCH_W].set(rw_w2[l]).at[RW_HEAD:LANES, BRANCH_W:].set(rw_a2[l])
        rw_prm = dict(mu=row(rw_mu[l]), w0=row(rw_w0[l]), wa2=wa2.astype(BF16), a0=row(rw_a0[l]),
                      g2=rw_g2[l].astype(BF16), kk=row(rw_kk[l]), ka=row(rw_ka[l]), rk=row(rw_rk[l]),
                      lnw=row(rw_lnw[l]), lnb=row(rw_lnb[l]), ones_bd=ones_bd)
        pp, ps = split(p_rw)
        o_p, sh_p, st_p = rwkv_branch(pp, jnp.zeros((bp, 1, RW_COLS), F32),
                                      jnp.zeros((bp, RW_HEADS, RW_HEAD, RW_HEAD), F32), rw_prm, min(RW_CHUNK, tp), tp)
        o_s, sh_s, st_s = rwkv_branch(ps, state_rw_shift[l][:, None, :], state_rwkv[l], rw_prm, dec_steps, 1)
        o_rw = join(o_p, o_s)
        outs["p_sh"].append(sh_p[:, 0])
        outs["p_rw"].append(st_p)
        outs["s_sh"].append(sh_s[:, 0])
        outs["s_rw"].append(st_s)

        gla_prm = dict(gw2=jnp.pad(gla_gw2[l], ((0, LANES - GLA_GATE_LORA), (0, 0))).astype(BF16),
                       gb=row(gla_gb[l]), lnw=row(gla_lnw[l]), lnb=row(gla_lnb[l]))
        pp, ps = split(p_gl)
        xp, xs_ = split(p_xg)
        o_p, st_p = gla_branch(pp, xp, jnp.zeros((bp, GLA_HEADS, GLA_HK, GLA_HV), F32), gla_prm,
                               min(GLA_CHUNK, tp), tp)
        o_s, st_s = gla_branch(ps, xs_, state_gla[l], gla_prm, dec_steps, 1)
        o_gla = join(o_p, o_s)
        outs["p_gla"].append(st_p)
        outs["s_gla"].append(st_s)

        pp, ps = split(p_cm)
        cm_prm = lambda c: dict(lnw=row(cm_lnw[l]), lnb=row(cm_lnb[l]), ws=cm_ws[l][:, :c, :c],
                                bt=jnp.transpose(cm_b[l])[:c])
        o_p, v_p = cmlp_branch(pp, cm_prm(cm_chunk_p), cm_chunk_p)
        o_s, v_s = cmlp_branch(ps, cm_prm(dec_steps), dec_steps)
        o_cm = join(o_p, o_s)
        outs["p_cm"].append(v_p[:, :tp - ((tp - 1) // CM_CHUNK) * CM_CHUNK])
        outs["s_cm"].append(v_s[:, 0:1])

        mix = branch_mix(o_rw, o_gla, o_cm, p_gt, w_branch[l].astype(BF16), TOKEN_TILE, 512)
        x = matmul(mix, w_out[l].astype(BF16), TOKEN_TILE, 1024, res=x)

        hf = rmsnorm(x, norm2[l], BF16, TOKEN_TILE)
        q = matmul(hf, peer_wq[l].astype(BF16), TOKEN_TILE, 1024)
        beta, thr, cc = peer_router(q, peer_keys[l].reshape(2 * PEER_HEADS, PEER_NKEYS, PEER_DH), TOKEN_TILE)
        x = peer_experts(hf, peer_u[l].astype(BF16), peer_v[l].astype(BF16), beta, thr, cc, x, TOKEN_TILE, 4)

    y = rmsnorm(x, norm_f, F32, TOKEN_TILE)
    st = lambda k: jnp.stack(outs[k])
    return (y[:n_p].reshape(bp, tp, d), y[n_p:].reshape(bs, 1, d), st("p_sh"), st("p_rw"), st("p_gla"), st("p_cm"),
            st("s_sh"), st("s_rw"), st("s_gla"), st("s_cm"))
```

```python
import functools
import math

import jax
import jax.numpy as jnp
from jax import lax
from jax.experimental import pallas as pl
from jax.experimental.pallas import tpu as pltpu

F32 = jnp.float32
BF16 = jnp.bfloat16

D_MODEL = 2048
DEPTH = 4
BRANCH_W = 1024
RW_HEAD = 64
RW_HEADS = 16
RW_PAIRS = RW_HEADS // 2
RW_COLS = 3328
RW_GN_EPS = 64e-5
RW_CHUNK = 64
GLA_HEADS = 4
GLA_HK = 128
GLA_HV = 256
GLA_DK = 512
GLA_GATE_LORA = 16
GLA_TAU = 16.0
GLA_CHUNK = 64
CM_W = 1024
CM_CHUNK = 128
CM_GROUPS = 8
CM_GW = 128
PEER_HEADS = 8
PEER_NKEYS = 128
PEER_N = PEER_NKEYS * PEER_NKEYS
PEER_DH = 128
PEER_TOPK = 16
PEER_ITILES = 16
PEER_SUB_KEYS = 2
RMS_EPS = 1e-6
LN_EPS = 1e-5

LANES = 128
SUBLANES = 8
VMEM_LIMIT_BYTES = 56 * 1024 * 1024

NEG_INF = float("-inf")

_NN = (((1,), (0,)), ((), ()))
_NT = (((1,), (1,)), ((), ()))
_TN = (((0,), (0,)), ((), ()))


def _params(*sem):
    return pltpu.CompilerParams(dimension_semantics=sem, vmem_limit_bytes=VMEM_LIMIT_BYTES)


def _bdot(a, b, dims=_NN):
    return lax.dot_general(a.astype(BF16), b.astype(BF16), dims, preferred_element_type=F32)


def _hdot(a, b, dims=_NN):
    return lax.dot_general(a.astype(F32), b.astype(F32), dims, preferred_element_type=F32,
                           precision=lax.Precision.HIGHEST)


def _gelu(x):
    return 0.5 * x * (1.0 + lax.erf(x * (1.0 / math.sqrt(2.0))))


def _sigmoid(x):
    return 1.0 / (1.0 + jnp.exp(-x))


def _softplus(x):
    return jnp.maximum(x, 0.0) + jnp.log(1.0 + jnp.exp(-jnp.abs(x)))


def _iota(shape, dim):
    return lax.broadcasted_iota(jnp.int32, shape, dim)


def _rmsnorm_kernel(x_ref, g_ref, o_ref):
    x = x_ref[...]
    ms = jnp.mean(x * x, axis=-1, keepdims=True)
    o_ref[...] = (x * lax.rsqrt(ms + RMS_EPS) * g_ref[...]).astype(o_ref.dtype)


def rmsnorm(x, g, out_dtype, tm):
    n, d = x.shape
    assert n % tm == 0
    return pl.pallas_call(
        _rmsnorm_kernel,
        grid=(n // tm,),
        in_specs=[pl.BlockSpec((tm, d), lambda i: (i, 0)), pl.BlockSpec((1, d), lambda i: (0, 0))],
        out_specs=pl.BlockSpec((tm, d), lambda i: (i, 0)),
        out_shape=jax.ShapeDtypeStruct((n, d), out_dtype),
        compiler_params=_params("parallel"),
        name="rmsnorm",
    )(x, g.reshape(1, d))


def _matmul_kernel(a_ref, w_ref, o_ref):
    o_ref[...] = jnp.dot(a_ref[...], w_ref[...], preferred_element_type=F32)


def _matmul_res_kernel(a_ref, w_ref, r_ref, o_ref):
    o_ref[...] = r_ref[...] + jnp.dot(a_ref[...], w_ref[...], preferred_element_type=F32)


def matmul(a, w, tm, tn, res=None):
    n, k = a.shape
    m = w.shape[1]
    assert n % tm == 0 and m % tn == 0
    in_specs = [pl.BlockSpec((tm, k), lambda j, i: (i, 0)), pl.BlockSpec((k, tn), lambda j, i: (0, j))]
    args = [a, w]
    body = _matmul_kernel
    if res is not None:
        in_specs.append(pl.BlockSpec((tm, tn), lambda j, i: (i, j)))
        args.append(res)
        body = _matmul_res_kernel
    return pl.pallas_call(
        body,
        grid=(m // tn, n // tm),
        in_specs=in_specs,
        out_specs=pl.BlockSpec((tm, tn), lambda j, i: (i, j)),
        out_shape=jax.ShapeDtypeStruct((n, m), F32),
        compiler_params=_params("parallel", "parallel"),
        name="matmul",
    )(*args)


def _split3(x):
    hi = x.astype(BF16)
    r1 = x - hi.astype(F32)
    mid = r1.astype(BF16)
    lo = (r1 - mid.astype(F32)).astype(BF16)
    return hi, mid, lo


def _exact_lhs_dot(a01, x, dims=_NN):
    a = a01.astype(BF16)
    hi, mid, lo = _split3(x)
    d = lambda y: lax.dot_general(a, y, dims, preferred_element_type=F32)
    return d(hi) + (d(mid) + d(lo))


def _exact_rhs_dot(x, b01, dims=_NN):
    b = b01.astype(BF16)
    hi, mid, lo = _split3(x)
    d = lambda y: lax.dot_general(y, b, dims, preferred_element_type=F32)
    return d(hi) + (d(mid) + d(lo))


def _chunk_cumsum(x, C):
    R, W = x.shape
    SL = max(C, 64)
    rr = _iota((SL, SL), 0)
    cc = _iota((SL, SL), 1)
    tri = ((rr // C == cc // C) & (rr >= cc)).astype(F32)
    same_chunk = (rr // C == cc // C).astype(F32)
    is_last = (_iota((SL, 1), 0) % C) == (C - 1)
    cums, ends = [], []
    for s0 in range(0, R, SL):
        cum_s = _exact_lhs_dot(tri, x[s0:s0 + SL])
        cums.append(cum_s)
        if C == SL:
            ends.append(jnp.broadcast_to(cum_s[C - 1:C], (SL, W)))
        else:
            ends.append(_exact_lhs_dot(same_chunk, jnp.where(is_last, cum_s, 0.0)))
    cum = jnp.concatenate(cums, axis=0) if len(cums) > 1 else cums[0]
    cum_end = jnp.concatenate(ends, axis=0) if len(ends) > 1 else ends[0]
    return cum, cum_end


def _pair_stack(x, lane_a):
    return jnp.concatenate([jnp.where(lane_a, x, 0.0), jnp.where(lane_a, 0.0, x)], axis=0)


def _pair_stack_cat(x, C):
    first = _iota((1, 2 * C), 1) < C
    return jnp.concatenate([jnp.where(first, x, 0.0), jnp.where(first, 0.0, x)], axis=0)


def _rwkv_kernel(p_ref, prev_ref, s0_ref, mu_ref, w0_ref, wa2_ref, a0_ref, g2_ref, kk_ref, ka_ref, rk_ref,
                 lnw_ref, lnb_ref, ones_ref, o_ref, shift_ref, sout_ref, s_scr, prev_scr, y_scr, *, C, NB, NC, t_valid):
    blk = pl.program_id(1)
    n_blk = pl.num_programs(1)
    H = RW_HEAD
    TB = NC * C
    R = NB * TB

    @pl.when(blk == 0)
    def _init():
        prev_scr[...] = prev_ref[:, 0, :]
        s_scr[...] = jnp.zeros_like(s_scr)
        for nb in range(NB):
            for j in range(RW_PAIRS):
                s_scr[nb, j, 0:H, 0:H] = s0_ref[nb, 2 * j]
                s_scr[nb, j, H:2 * H, H:2 * H] = s0_ref[nb, 2 * j + 1]

    p = p_ref[...].reshape(R, RW_COLS)
    t_loc = _iota((R, 1), 0) % TB
    carried = jnp.broadcast_to(prev_scr[...][:, None, :], (NB, TB, RW_COLS)).reshape(R, RW_COLS)
    prev = jnp.where(t_loc == 0, carried, pltpu.roll(p, 1, axis=0))
    last_row = (t_valid - 1) % TB
    prev_scr[...] = p_ref[:, last_row, :]
    xs = p + (prev - p) * mu_ref[...]
    r = xs[:, 0:1024]
    k = xs[:, 1024:2048]
    v = xs[:, 2048:3072]
    z = xs[:, 3072:3200]
    xg = xs[:, 3200:3328]
    lane128 = _iota((1, LANES), 1)
    tz = jnp.where(lane128 < RW_HEAD, jnp.tanh(z), z)
    wa = _bdot(tz, wa2_ref[...])
    w_lin = w0_ref[...] + wa[:, 0:1024]
    log_w = -jnp.exp(-_softplus(-w_lin) - 0.5)
    a = _sigmoid(a0_ref[...] + wa[:, 1024:2048])
    g = _bdot(_sigmoid(xg), g2_ref[...])
    ones_bd = ones_ref[...]

    def headsum(x):
        return jnp.concatenate(
            [_exact_rhs_dot(x[:, LANES * j:LANES * (j + 1)], ones_bd) for j in range(RW_PAIRS)], axis=1)

    kk = k * kk_ref[...]
    k_hat = k * (1.0 + (a - 1.0) * ka_ref[...])
    sums = headsum(jnp.concatenate([kk * kk, r * k_hat * rk_ref[...]], axis=0))
    kk = kk * lax.rsqrt(jnp.maximum(sums[0:R], 1e-24))
    bonus = sums[R:2 * R] * v
    b = kk * a
    if t_valid % TB != 0:
        valid = (blk * TB + t_loc) < t_valid
        log_w = jnp.where(valid, log_w, 0.0)
        kk = jnp.where(valid, kk, 0.0)
        b = jnp.where(valid, b, 0.0)
        k_hat = jnp.where(valid, k_hat, 0.0)
    cum, cum_end = _chunk_cumsum(log_w, C)
    kap_t = kk * jnp.exp(cum - log_w)
    r_t = r * jnp.exp(cum)
    inv = jnp.exp(-cum)
    b_t = b * inv
    k_t = k_hat * inv
    to_end = jnp.exp(cum_end - cum)
    b_e = b * to_end
    k_e = k_hat * to_end
    p_end = jnp.exp(cum_end)

    lane_a = lane128 < RW_HEAD
    tt = _iota((C, 2 * C), 0)
    ss = _iota((C, 2 * C), 1) % C
    strict = tt > ss
    lower = tt >= ss
    eye_cat = (tt == ss).astype(F32)
    blockdiag = (_iota((LANES, LANES), 0) < RW_HEAD) == (_iota((LANES, LANES), 1) < RW_HEAD)

    units = [(nb, ci, j) for nb in range(NB) for ci in range(NC) for j in range(RW_PAIRS)]

    def rows_of(nb, ci):
        r0 = (nb * NC + ci) * C
        return slice(r0, r0 + C)

    def lanes_of(j):
        return slice(LANES * j, LANES * (j + 1))

    kr, a1, a2, a3, a4, tm = {}, {}, {}, {}, {}, {}
    for un in units:
        rs, sl = rows_of(un[0], un[1]), lanes_of(un[2])
        kr[un] = jnp.concatenate([kap_t[rs, sl], r_t[rs, sl]], axis=0)
    for un in units:
        rs, sl = rows_of(un[0], un[1]), lanes_of(un[2])
        g1 = _bdot(kr[un], _pair_stack(b_t[rs, sl], lane_a), _NT)
        a1[un] = jnp.where(strict, g1[0:C], 0.0)
        a3[un] = jnp.where(lower, g1[C:2 * C], 0.0)
    for un in units:
        rs, sl = rows_of(un[0], un[1]), lanes_of(un[2])
        g2 = _bdot(kr[un], _pair_stack(k_t[rs, sl], lane_a), _NT)
        a2[un] = jnp.where(strict, g2[0:C], 0.0)
        a4[un] = jnp.where(lower, g2[C:2 * C], 0.0)
    for un in units:
        tm[un] = eye_cat - jnp.where((tt // 2 == ss // 2), a1[un], 0.0)
    m = 2
    while m < C:
        sel = (tt // (2 * m) == ss // (2 * m)) & ((tt // m) % 2 == 1) & ((ss // m) % 2 == 0)
        inner = {}
        for un in units:
            inner[un] = _bdot(jnp.where(sel, a1[un], 0.0), _pair_stack_cat(tm[un], C))
        for un in units:
            tm[un] = tm[un] - _bdot(tm[un], _pair_stack_cat(inner[un], C))
        m *= 2
    av = {}
    for un in units:
        rs, sl = rows_of(un[0], un[1]), lanes_of(un[2])
        av[un] = _bdot(jnp.concatenate([a2[un], a4[un]], axis=0), _pair_stack(v[rs, sl], lane_a))
    for ci in range(NC):
        cur = [(nb, ci, j) for nb in range(NB) for j in range(RW_PAIRS)]
        xs_s, u = {}, {}
        for un in cur:
            xs_s[un] = _bdot(kr[un], s_scr[un[0], un[2]], _NT)
        for un in cur:
            u[un] = -_bdot(tm[un], _pair_stack(xs_s[un][0:C] + av[un][0:C], lane_a))
        for un in cur:
            rs, sl = rows_of(un[0], un[1]), lanes_of(un[2])
            ds = _bdot(jnp.concatenate([u[un], v[rs, sl]], axis=0),
                       jnp.concatenate([b_e[rs, sl], k_e[rs, sl]], axis=0), _TN)
            s_scr[un[0], un[2]] = (s_scr[un[0], un[2]] * p_end[rs.start:rs.start + 1, sl]
                                   + jnp.where(blockdiag, ds, 0.0))
        for un in cur:
            rs, sl = rows_of(un[0], un[1]), lanes_of(un[2])
            y_scr[rs, sl] = (xs_s[un][C:2 * C] + av[un][C:2 * C]
                             + _bdot(a3[un], _pair_stack(u[un], lane_a)))

    y = y_scr[...]
    mean = headsum(y) * (1.0 / RW_HEAD)
    yc = y - mean
    var = headsum(yc * yc) * (1.0 / RW_HEAD)
    yn = yc * lax.rsqrt(var + RW_GN_EPS) * lnw_ref[...] + lnb_ref[...]
    o_ref[...] = ((yn + bonus) * g).reshape(NB, TB, BRANCH_W)

    @pl.when(blk == n_blk - 1)
    def _fin():
        shift_ref[:, 0, :] = p_ref[:, last_row, :]
        for nb in range(NB):
            for j in range(RW_PAIRS):
                sout_ref[nb, 2 * j] = s_scr[nb, j, 0:H, 0:H]
                sout_ref[nb, 2 * j + 1] = s_scr[nb, j, H:2 * H, H:2 * H]


def rwkv_branch(p3, shift_prev, s0, prm, C, NB, NC, t_valid):
    B, T, _ = p3.shape
    TB = NC * C
    assert T % TB == 0 and B % NB == 0
    row = lambda w: pl.BlockSpec((1, w), lambda b, c: (0, 0))
    full = lambda a: pl.BlockSpec(a.shape, lambda b, c: (0,) * a.ndim)
    kern = functools.partial(_rwkv_kernel, C=C, NB=NB, NC=NC, t_valid=t_valid)
    return pl.pallas_call(
        kern,
        grid=(B // NB, T // TB),
        in_specs=[
            pl.BlockSpec((NB, TB, RW_COLS), lambda b, c: (b, c, 0)),
            pl.BlockSpec((NB, 1, RW_COLS), lambda b, c: (b, 0, 0)),
            pl.BlockSpec((NB, RW_HEADS, RW_HEAD, RW_HEAD), lambda b, c: (b, 0, 0, 0)),
            row(RW_COLS), row(1024), full(prm["wa2"]), row(1024), full(prm["g2"]),
            row(1024), row(1024), row(1024), row(1024), row(1024), full(prm["ones_bd"]),
        ],
        out_specs=[
            pl.BlockSpec((NB, TB, BRANCH_W), lambda b, c: (b, c, 0)),
            pl.BlockSpec((NB, 1, RW_COLS), lambda b, c: (b, 0, 0)),
            pl.BlockSpec((NB, RW_HEADS, RW_HEAD, RW_HEAD), lambda b, c: (b, 0, 0, 0)),
        ],
        out_shape=[
            jax.ShapeDtypeStruct((B, T, BRANCH_W), F32),
            jax.ShapeDtypeStruct((B, 1, RW_COLS), F32),
            jax.ShapeDtypeStruct((B, RW_HEADS, RW_HEAD, RW_HEAD), F32),
        ],
        scratch_shapes=[
            pltpu.VMEM((NB, RW_PAIRS, LANES, LANES), F32),
            pltpu.VMEM((NB, RW_COLS), F32),
            pltpu.VMEM((NB * TB, BRANCH_W), F32),
        ],
        compiler_params=_params("parallel", "arbitrary"),
        name="rwkv7",
    )(p3, shift_prev, s0, prm["mu"], prm["w0"], prm["wa2"], prm["a0"], prm["g2"], prm["kk"], prm["ka"],
      prm["rk"], prm["lnw"], prm["lnb"], prm["ones_bd"])


def _layernorm_lanes(x, w, b, eps):
    mean = jnp.mean(x, axis=-1, keepdims=True)
    xc = x - mean
    var = jnp.mean(xc * xc, axis=-1, keepdims=True)
    return xc * lax.rsqrt(var + eps) * w + b


def _gla_kernel(pm_ref, pxg_ref, s0_ref, gw2_ref, gb_ref, lnw_ref, lnb_ref, o_ref, sout_ref, s_scr, *, C, NB, NC, t_valid):
    blk = pl.program_id(1)
    n_blk = pl.num_programs(1)
    TB = NC * C
    R = NB * TB

    @pl.when(blk == 0)
    def _init():
        s_scr[...] = s0_ref[...]

    pm = pm_ref[...].reshape(R, 3 * BRANCH_W)
    q = pm[:, 0:GLA_DK] * (GLA_HK ** -0.5)
    k = pm[:, GLA_DK:2 * GLA_DK]
    v = pm[:, 1024:2048]
    r = pm[:, 2048:3072]
    zl = _bdot(pxg_ref[...].reshape(R, LANES), gw2_ref[...]) + gb_ref[...]
    log_a = (jnp.minimum(zl, 0.0) - jnp.log(1.0 + jnp.exp(-jnp.abs(zl)))) * (1.0 / GLA_TAU)
    if t_valid % TB != 0:
        t_loc = _iota((R, 1), 0) % TB
        log_a = jnp.where((blk * TB + t_loc) < t_valid, log_a, 0.0)
    cum, cum_end = _chunk_cumsum(log_a, C)
    qe = q * jnp.exp(cum)
    ke = k * jnp.exp(-cum)
    kd = k * jnp.exp(cum_end - cum)
    decay = jnp.exp(cum_end)
    silu_r = r * _sigmoid(r)
    tri = _iota((C, C), 0) >= _iota((C, C), 1)

    units = [(nb, ci, h) for nb in range(NB) for ci in range(NC) for h in range(GLA_HEADS)]
    rows_of = lambda nb, ci: slice((nb * NC + ci) * C, (nb * NC + ci + 1) * C)
    kl_of = lambda h: slice(GLA_HK * h, GLA_HK * (h + 1))
    vl_of = lambda h: slice(GLA_HV * h, GLA_HV * (h + 1))
    a_mat, o_in, ds = {}, {}, {}
    for un in units:
        rs, kl = rows_of(un[0], un[1]), kl_of(un[2])
        a_mat[un] = jnp.where(tri, _bdot(qe[rs, kl], ke[rs, kl], _NT), 0.0)
    for un in units:
        rs, vl = rows_of(un[0], un[1]), vl_of(un[2])
        o_in[un] = _bdot(a_mat[un], v[rs, vl])
    for un in units:
        rs, kl, vl = rows_of(un[0], un[1]), kl_of(un[2]), vl_of(un[2])
        ds[un] = _bdot(kd[rs, kl], v[rs, vl], _TN)
    for ci in range(NC):
        cur = [(nb, ci, h) for nb in range(NB) for h in range(GLA_HEADS)]
        o = {}
        for un in cur:
            rs, kl = rows_of(un[0], un[1]), kl_of(un[2])
            o[un] = o_in[un] + _bdot(qe[rs, kl], s_scr[un[0], un[2]])
        for un in cur:
            rs, kl = rows_of(un[0], un[1]), kl_of(un[2])
            dcol = jnp.transpose(jnp.broadcast_to(decay[rs.start:rs.start + 1, kl], (GLA_HK, GLA_HK)))
            s_scr[un[0], un[2]] = s_scr[un[0], un[2]] * jnp.concatenate([dcol, dcol], axis=1) + ds[un]
        for un in cur:
            rs, vl = rows_of(un[0], un[1]), vl_of(un[2])
            res = _layernorm_lanes(o[un], lnw_ref[:, vl], lnb_ref[:, vl], LN_EPS) * silu_r[rs, vl]
            o_ref[un[0], un[1] * C:(un[1] + 1) * C, vl] = res

    @pl.when(blk == n_blk - 1)
    def _fin():
        sout_ref[...] = s_scr[...]


def gla_branch(pm3, pxg3, s0, prm, C, NB, NC, t_valid):
    B, T, _ = pm3.shape
    TB = NC * C
    assert T % TB == 0 and B % NB == 0
    row = lambda w: pl.BlockSpec((1, w), lambda b, c: (0, 0))
    kern = functools.partial(_gla_kernel, C=C, NB=NB, NC=NC, t_valid=t_valid)
    return pl.pallas_call(
        kern,
        grid=(B // NB, T // TB),
        in_specs=[
            pl.BlockSpec((NB, TB, 3 * BRANCH_W), lambda b, c: (b, c, 0)),
            pl.BlockSpec((NB, TB, LANES), lambda b, c: (b, c, 0)),
            pl.BlockSpec((NB, GLA_HEADS, GLA_HK, GLA_HV), lambda b, c: (b, 0, 0, 0)),
            pl.BlockSpec((LANES, GLA_DK), lambda b, c: (0, 0)),
            row(GLA_DK), row(BRANCH_W), row(BRANCH_W),
        ],
        out_specs=[
            pl.BlockSpec((NB, TB, BRANCH_W), lambda b, c: (b, c, 0)),
            pl.BlockSpec((NB, GLA_HEADS, GLA_HK, GLA_HV), lambda b, c: (b, 0, 0, 0)),
        ],
        out_shape=[
            jax.ShapeDtypeStruct((B, T, BRANCH_W), F32),
            jax.ShapeDtypeStruct((B, GLA_HEADS, GLA_HK, GLA_HV), F32),
        ],
        scratch_shapes=[pltpu.VMEM((NB, GLA_HEADS, GLA_HK, GLA_HV), F32)],
        compiler_params=_params("parallel", "arbitrary"),
        name="gla",
    )(pm3, pxg3, s0, prm["gw2"], prm["gb"], prm["lnw"], prm["lnb"])


def _cmlp_kernel(p_ref, lnw_ref, lnb_ref, ws_ref, bt_ref, o_ref, v_ref, *, C, NB):
    c = pl.program_id(1)
    n_chunks = pl.num_programs(1)
    R = NB * C
    z = _gelu(p_ref[...].reshape(R, 2 * CM_W))
    u = z[:, 0:CM_W]
    vn = _layernorm_lanes(z[:, CM_W:2 * CM_W], lnw_ref[...], lnb_ref[...], LN_EPS)
    tri = _iota((C, C), 0) >= _iota((C, C), 1)
    for g in range(CM_GROUPS):
        gl = slice(CM_GW * g, CM_GW * (g + 1))
        ws_c = jnp.where(tri, ws_ref[g], 0.0)
        for nb in range(NB):
            rs = slice(nb * C, (nb + 1) * C)
            mixed = _bdot(ws_c, vn[rs, gl]) + bt_ref[:, g:g + 1]
            o_ref[nb, :, gl] = u[rs, gl] * mixed

    @pl.when(c == n_chunks - 1)
    def _fin():
        v_ref[...] = vn.reshape(NB, C, CM_W)


def cmlp_branch(p3, prm, C, NB):
    B, T, _ = p3.shape
    assert T % C == 0 and B % NB == 0
    row = lambda w: pl.BlockSpec((1, w), lambda b, c: (0, 0))
    return pl.pallas_call(
        functools.partial(_cmlp_kernel, C=C, NB=NB),
        grid=(B // NB, T // C),
        in_specs=[
            pl.BlockSpec((NB, C, 2 * CM_W), lambda b, c: (b, c, 0)),
            row(CM_W), row(CM_W),
            pl.BlockSpec((CM_GROUPS, C, C), lambda b, c: (0, 0, 0)),
            pl.BlockSpec((C, CM_GROUPS), lambda b, c: (0, 0)),
        ],
        out_specs=[
            pl.BlockSpec((NB, C, CM_W), lambda b, c: (b, c, 0)),
            pl.BlockSpec((NB, C, CM_W), lambda b, c: (b, 0, 0)),
        ],
        out_shape=[
            jax.ShapeDtypeStruct((B, T, CM_W), F32),
            jax.ShapeDtypeStruct((B, C, CM_W), F32),
        ],
        compiler_params=_params("parallel", "arbitrary"),
        name="cmlp",
    )(p3, prm["lnw"], prm["lnb"], prm["ws"], prm["bt"])


def _mix_kernel(orw_ref, ogl_ref, ocm_ref, g0_ref, g1_ref, g2_ref, wb_ref, o_ref):
    acc = _sigmoid(g0_ref[...]) * _bdot(orw_ref[...], wb_ref[0])
    acc += _sigmoid(g1_ref[...]) * _bdot(ogl_ref[...], wb_ref[1])
    acc += _sigmoid(g2_ref[...]) * _bdot(ocm_ref[...], wb_ref[2])
    o_ref[...] = acc.astype(o_ref.dtype)


def branch_mix(o_rw, o_gla, o_cm, p_gate, wb, tm, tn):
    n = o_rw.shape[0]
    nj = D_MODEL // tn
    ospec = pl.BlockSpec((tm, BRANCH_W), lambda i, j: (i, 0))
    gspec = lambda g: pl.BlockSpec((tm, tn), lambda i, j: (i, g * nj + j))
    return pl.pallas_call(
        _mix_kernel,
        grid=(n // tm, nj),
        in_specs=[ospec, ospec, ospec, gspec(0), gspec(1), gspec(2),
                  pl.BlockSpec((3, BRANCH_W, tn), lambda i, j: (0, 0, j))],
        out_specs=pl.BlockSpec((tm, tn), lambda i, j: (i, j)),
        out_shape=jax.ShapeDtypeStruct((n, D_MODEL), BF16),
        compiler_params=_params("parallel", "parallel"),
        name="branch_mix",
    )(o_rw, o_gla, o_cm, p_gate, p_gate, p_gate, wb)


def _top_rows(s, n):
    out = []
    for _ in range(n):
        m = jnp.max(s, axis=0, keepdims=True)
        out.append(m)
        s = jnp.where(s == m, NEG_INF, s)
    return out


def _router_kernel(q_ref, keys_ref, beta_ref, thr_ref, c_ref):
    tn = q_ref.shape[0]
    n_top = PEER_TOPK + 1
    for h in range(PEER_HEADS):
        s1 = _hdot(keys_ref[2 * h], q_ref[:, LANES * (2 * h):LANES * (2 * h + 1)], _NT)
        s2 = _hdot(keys_ref[2 * h + 1], q_ref[:, LANES * (2 * h + 1):LANES * (2 * h + 2)], _NT)
        a = _top_rows(s1, n_top)
        b = _top_rows(s2, n_top)
        pad = jnp.full((3 * SUBLANES - n_top, tn), NEG_INF, F32)
        b_all = jnp.concatenate(b + [pad], axis=0)
        cand = jnp.concatenate([a[0] + b_all] + [a[x] + b_all[0:SUBLANES] for x in range(1, n_top)], axis=0)
        best = _top_rows(cand, n_top)
        top = best[0]
        zsum = jnp.zeros_like(top)
        for t in range(PEER_TOPK):
            zsum += jnp.exp(best[t] - top)
        tau = 0.5 * (best[PEER_TOPK - 1] + best[PEER_TOPK])
        rows = slice(SUBLANES * h, SUBLANES * (h + 1))
        beta_ref[h] = s2 - b[0]
        thr_ref[:, rows, :] = ((tau - top) - (s1 - a[0])).reshape(PEER_ITILES, SUBLANES, tn)
        c_ref[:, rows, :] = (jnp.exp(s1 - a[0]) / zsum).reshape(PEER_ITILES, SUBLANES, tn)


def peer_router(q, keys, tn):
    n = q.shape[0]
    spec = pl.BlockSpec((PEER_HEADS, PEER_NKEYS, tn), lambda i: (0, 0, i))
    shp = jax.ShapeDtypeStruct((PEER_HEADS, PEER_NKEYS, n), F32)
    spec_i = pl.BlockSpec((PEER_ITILES, PEER_HEADS * SUBLANES, tn), lambda i: (0, 0, i))
    shp_i = jax.ShapeDtypeStruct((PEER_ITILES, PEER_HEADS * SUBLANES, n), F32)
    return pl.pallas_call(
        _router_kernel,
        grid=(n // tn,),
        in_specs=[pl.BlockSpec((tn, D_MODEL), lambda i: (i, 0)),
                  pl.BlockSpec((2 * PEER_HEADS, PEER_NKEYS, PEER_DH), lambda i: (0, 0, 0))],
        out_specs=[spec, spec_i, spec_i],
        out_shape=[shp, shp_i, shp_i],
        compiler_params=_params("parallel"),
        name="peer_router",
    )(q, keys)


def _peer_kernel(xn_ref, u_ref, v_ref, beta_ref, thr_ref, c_ref, res_ref, o_ref, eb_scr, *, ti):
    e = pl.program_id(1)
    sub_i = PEER_SUB_KEYS

    @pl.when(e == 0)
    def _init():
        o_ref[...] = res_ref[...]
        eb_scr[...] = jnp.exp(beta_ref[...])

    sub = (e % (SUBLANES // ti)) * ti
    xn = xn_ref[...]
    starts = list(range(0, ti, sub_i))

    def scores(s0):
        rows = slice(s0 * PEER_NKEYS, (s0 + sub_i) * PEER_NKEYS)
        return lax.dot_general(u_ref[rows, :], xn, _NT, preferred_element_type=F32)

    def weights(s0, ht):
        gates = []
        for ii in range(s0, s0 + sub_i):
            acc = None
            for h in range(PEER_HEADS):
                r = pl.ds(SUBLANES * h + sub + ii, 1)
                hit = beta_ref[h] >= thr_ref[0, r, :]
                term = jnp.where(hit, eb_scr[h] * c_ref[0, r, :], 0.0)
                acc = term if acc is None else acc + term
            gates.append(acc)
        g = jnp.concatenate(gates, axis=0) if len(gates) > 1 else gates[0]
        return (g * _gelu(ht)).astype(BF16)

    tot = None
    ht_next = scores(starts[0])
    for n, s0 in enumerate(starts):
        ht = ht_next
        if n + 1 < len(starts):
            ht_next = scores(starts[n + 1])
        rows = slice(s0 * PEER_NKEYS, (s0 + sub_i) * PEER_NKEYS)
        part = lax.dot_general(weights(s0, ht), v_ref[rows, :], _TN, preferred_element_type=F32)
        tot = part if tot is None else tot + part
    o_ref[...] += tot


def peer_experts(xn, u, v, beta, thr, cc, res, tm, ti):
    n = xn.shape[0]
    te = ti * PEER_NKEYS
    per = SUBLANES // ti
    once = dict(pipeline_mode=pl.Buffered(1))
    ispec = pl.BlockSpec((1, PEER_HEADS * SUBLANES, tm), lambda i, e: (e // per, 0, i))
    return pl.pallas_call(
        functools.partial(_peer_kernel, ti=ti),
        grid=(n // tm, PEER_NKEYS // ti),
        in_specs=[
            pl.BlockSpec((tm, D_MODEL), lambda i, e: (i, 0), **once),
            pl.BlockSpec((te, D_MODEL), lambda i, e: (e, 0)),
            pl.BlockSpec((te, D_MODEL), lambda i, e: (e, 0)),
            pl.BlockSpec((PEER_HEADS, PEER_NKEYS, tm), lambda i, e: (0, 0, i), **once),
            ispec, ispec,
            pl.BlockSpec((tm, D_MODEL), lambda i, e: (i, 0), **once),
        ],
        out_specs=pl.BlockSpec((tm, D_MODEL), lambda i, e: (i, 0)),
        out_shape=jax.ShapeDtypeStruct((n, D_MODEL), F32),
        scratch_shapes=[pltpu.VMEM((PEER_HEADS, PEER_NKEYS, tm), F32)],
        compiler_params=_params("parallel", "arbitrary"),
        name="peer_experts",
    )(xn, u, v, beta, thr, cc, res)


TOKEN_TILE = 640
PROMPT_CHUNKS_PER_STEP = 4
DECODE_ROWS_PER_STEP = 8
PEER_KEYS_PER_STEP = 8
W_IN_COLS = dict(rw=(0, 3328), gla_qkv=(3328, 5376), gla_xg=(5376, 5392), gla_r=(5392, 6416),
                 cm=(6416, 8464), gate=(8464, 14608))


def _proj(hn, w, tn):
    return matmul(hn, w.astype(BF16), TOKEN_TILE, tn)


def _pad_steps(a, steps):
    return jnp.pad(a, ((0, 0), (0, steps - a.shape[1]), (0, 0)))


def kernel(x_prompt, x_sample, state_rw_shift, state_rwkv, state_gla, norm1, w_in, rw_mu, rw_w0, rw_w2, rw_a0, rw_a2, rw_g2, rw_kk, rw_ka, rw_rk, rw_lnw, rw_lnb, gla_gw2, gla_gb, gla_lnw, gla_lnb, cm_lnw, cm_lnb, cm_ws, cm_b, w_branch, w_out, norm2, peer_wq, peer_keys, peer_u, peer_v, norm_f):
    bp, tp, d = x_prompt.shape
    bs = x_sample.shape[0]
    n_p = bp * tp
    x = jnp.concatenate([x_prompt.reshape(n_p, d), x_sample.reshape(bs, d)], axis=0)
    head_of_lane = jnp.arange(LANES) // RW_HEAD
    ones_bd = (head_of_lane[:, None] == head_of_lane[None, :]).astype(F32)
    row = lambda a: a.reshape(1, -1)
    dec_steps = SUBLANES
    cm_chunk_p = min(CM_CHUNK, tp)
    outs = {k: [] for k in ("p_sh", "p_rw", "p_gla", "p_cm", "s_sh", "s_rw", "s_gla", "s_cm")}
    for l in range(DEPTH):
        wl = w_in[l]
        cols = lambda name: wl[:, W_IN_COLS[name][0]:W_IN_COLS[name][1]]
        hn = rmsnorm(x, norm1[l], BF16, TOKEN_TILE)
        p_rw = _proj(hn, cols("rw"), 1664)
        p_gl = _proj(hn, jnp.concatenate([cols("gla_qkv"), cols("gla_r")], axis=1), 1536)
        p_xg = _proj(hn, jnp.pad(cols("gla_xg"), ((0, 0), (0, LANES - GLA_GATE_LORA))), LANES)
        p_cm = _proj(hn, cols("cm"), 1024)
        p_gt = _proj(hn, cols("gate"), 1536)

        split = lambda p: (p[:n_p].reshape(bp, tp, -1), _pad_steps(p[n_p:].reshape(bs, 1, -1), dec_steps))
        join = lambda a, b: jnp.concatenate([a.reshape(n_p, -1), b[:, 0, :]], axis=0)

        wa2 = jnp.zeros((LANES, 2 * BRANCH_W), F32)
        wa2 = wa2.at[0:RW_HEAD, 0:BRANCH_W].set(rw_w2[l]).at[RW_HEAD:LANES, BRANCH_W:].set(rw_a2[l])
        rw_prm = dict(mu=row(rw_mu[l]), w0=row(rw_w0[l]), wa2=wa2.astype(BF16), a0=row(rw_a0[l]),
                      g2=rw_g2[l].astype(BF16), kk=row(rw_kk[l]), ka=row(rw_ka[l]), rk=row(rw_rk[l]),
                      lnw=row(rw_lnw[l]), lnb=row(rw_lnb[l]), ones_bd=ones_bd)
        pp, ps = split(p_rw)
        o_p, sh_p, st_p = rwkv_branch(pp, jnp.zeros((bp, 1, RW_COLS), F32),
                                      jnp.zeros((bp, RW_HEADS, RW_HEAD, RW_HEAD), F32), rw_prm,
                                      RW_CHUNK, 1, PROMPT_CHUNKS_PER_STEP, tp)
        o_s, sh_s, st_s = rwkv_branch(ps, state_rw_shift[l][:, None, :], state_rwkv[l], rw_prm,
                                      dec_steps, DECODE_ROWS_PER_STEP, 1, 1)
        o_rw = join(o_p, o_s)
        outs["p_sh"].append(sh_p[:, 0])
        outs["p_rw"].append(st_p)
        outs["s_sh"].append(sh_s[:, 0])
        outs["s_rw"].append(st_s)

        gla_prm = dict(gw2=jnp.pad(gla_gw2[l], ((0, LANES - GLA_GATE_LORA), (0, 0))).astype(BF16),
                       gb=row(gla_gb[l]), lnw=row(gla_lnw[l]), lnb=row(gla_lnb[l]))
        pp, ps = split(p_gl)
        xp, xs_ = split(p_xg)
        o_p, st_p = gla_branch(pp, xp, jnp.zeros((bp, GLA_HEADS, GLA_HK, GLA_HV), F32), gla_prm,
                               GLA_CHUNK, 1, PROMPT_CHUNKS_PER_STEP, tp)
        o_s, st_s = gla_branch(ps, xs_, state_gla[l], gla_prm, dec_steps, DECODE_ROWS_PER_STEP, 1, 1)
        o_gla = join(o_p, o_s)
        outs["p_gla"].append(st_p)
        outs["s_gla"].append(st_s)

        pp, ps = split(p_cm)
        cm_prm = lambda c: dict(lnw=row(cm_lnw[l]), lnb=row(cm_lnb[l]), ws=cm_ws[l][:, :c, :c],
                                bt=jnp.transpose(cm_b[l])[:c])
        o_p, v_p = cmlp_branch(pp, cm_prm(cm_chunk_p), cm_chunk_p, 1)
        o_s, v_s = cmlp_branch(ps, cm_prm(dec_steps), dec_steps, DECODE_ROWS_PER_STEP)
        o_cm = join(o_p, o_s)
        outs["p_cm"].append(v_p[:, :tp - ((tp - 1) // CM_CHUNK) * CM_CHUNK])
        outs["s_cm"].append(v_s[:, 0:1])

        mix = branch_mix(o_rw, o_gla, o_cm, p_gt, w_branch[l].astype(BF16), TOKEN_TILE, 512)
        x = matmul(mix, w_out[l].astype(BF16), TOKEN_TILE, 1024, res=x)

        hf = rmsnorm(x, norm2[l], BF16, TOKEN_TILE)
        q = matmul(hf, peer_wq[l].astype(BF16), TOKEN_TILE, 1024)
        beta, thr, cc = peer_router(q, peer_keys[l].reshape(2 * PEER_HEADS, PEER_NKEYS, PEER_DH), TOKEN_TILE)
        x = peer_experts(hf, peer_u[l].astype(BF16), peer_v[l].astype(BF16), beta, thr, cc, x, TOKEN_TILE,
                         PEER_KEYS_PER_STEP)

    y = rmsnorm(x, norm_f, F32, TOKEN_TILE)
    st = lambda k: jnp.stack(outs[k])
    return (y[:n_p].reshape(bp, tp, d), y[n_p:].reshape(bs, 1, d), st("p_sh"), st("p_rw"), st("p_gla"), st("p_cm"),
            st("s_sh"), st("s_rw"), st("s_gla"), st("s_cm"))
```

```python
import functools
import math

import jax
import jax.numpy as jnp
from jax import lax
from jax.experimental import pallas as pl
from jax.experimental.pallas import tpu as pltpu

F32 = jnp.float32
BF16 = jnp.bfloat16

D_MODEL = 2048
DEPTH = 4
BRANCH_W = 1024
RW_HEAD = 64
RW_HEADS = 16
RW_PAIRS = RW_HEADS // 2
RW_COLS = 3328
RW_GN_EPS = 64e-5
RW_CHUNK = 64
GLA_HEADS = 4
GLA_HK = 128
GLA_HV = 256
GLA_DK = 512
GLA_GATE_LORA = 16
GLA_TAU = 16.0
GLA_CHUNK = 64
CM_W = 1024
CM_CHUNK = 128
CM_GROUPS = 8
CM_GW = 128
PEER_HEADS = 8
PEER_NKEYS = 128
PEER_N = PEER_NKEYS * PEER_NKEYS
PEER_DH = 128
PEER_TOPK = 16
PEER_ITILES = 16
PEER_SUB_KEYS = 2
RMS_EPS = 1e-6
LN_EPS = 1e-5

LANES = 128
SUBLANES = 8
VMEM_LIMIT_BYTES = 56 * 1024 * 1024

NEG_INF = float("-inf")

_NN = (((1,), (0,)), ((), ()))
_NT = (((1,), (1,)), ((), ()))
_TN = (((0,), (0,)), ((), ()))


def _params(*sem):
    return pltpu.CompilerParams(dimension_semantics=sem, vmem_limit_bytes=VMEM_LIMIT_BYTES)


def _bdot(a, b, dims=_NN):
    return lax.dot_general(a.astype(BF16), b.astype(BF16), dims, preferred_element_type=F32)


def _hdot(a, b, dims=_NN):
    return lax.dot_general(a.astype(F32), b.astype(F32), dims, preferred_element_type=F32,
                           precision=lax.Precision.HIGHEST)


def _gelu(x):
    return 0.5 * x * (1.0 + lax.erf(x * (1.0 / math.sqrt(2.0))))


def _sigmoid(x):
    return 1.0 / (1.0 + jnp.exp(-x))


def _softplus(x):
    return jnp.maximum(x, 0.0) + jnp.log(1.0 + jnp.exp(-jnp.abs(x)))


def _iota(shape, dim):
    return lax.broadcasted_iota(jnp.int32, shape, dim)


def _rmsnorm_kernel(x_ref, g_ref, o_ref):
    x = x_ref[...]
    ms = jnp.mean(x * x, axis=-1, keepdims=True)
    o_ref[...] = (x * lax.rsqrt(ms + RMS_EPS) * g_ref[...]).astype(o_ref.dtype)


def rmsnorm(x, g, out_dtype, tm):
    n, d = x.shape
    assert n % tm == 0
    return pl.pallas_call(
        _rmsnorm_kernel,
        grid=(n // tm,),
        in_specs=[pl.BlockSpec((tm, d), lambda i: (i, 0)), pl.BlockSpec((1, d), lambda i: (0, 0))],
        out_specs=pl.BlockSpec((tm, d), lambda i: (i, 0)),
        out_shape=jax.ShapeDtypeStruct((n, d), out_dtype),
        compiler_params=_params("parallel"),
        name="rmsnorm",
    )(x, g.reshape(1, d))


def _matmul_kernel(a_ref, w_ref, o_ref):
    o_ref[...] = jnp.dot(a_ref[...], w_ref[...], preferred_element_type=F32)


def _matmul_res_kernel(a_ref, w_ref, r_ref, o_ref):
    o_ref[...] = r_ref[...] + jnp.dot(a_ref[...], w_ref[...], preferred_element_type=F32)


def matmul(a, w, tm, tn, res=None):
    n, k = a.shape
    m = w.shape[1]
    assert n % tm == 0 and m % tn == 0
    in_specs = [pl.BlockSpec((tm, k), lambda j, i: (i, 0)), pl.BlockSpec((k, tn), lambda j, i: (0, j))]
    args = [a, w]
    body = _matmul_kernel
    if res is not None:
        in_specs.append(pl.BlockSpec((tm, tn), lambda j, i: (i, j)))
        args.append(res)
        body = _matmul_res_kernel
    return pl.pallas_call(
        body,
        grid=(m // tn, n // tm),
        in_specs=in_specs,
        out_specs=pl.BlockSpec((tm, tn), lambda j, i: (i, j)),
        out_shape=jax.ShapeDtypeStruct((n, m), F32),
        compiler_params=_params("parallel", "parallel"),
        name="matmul",
    )(*args)


def _split3(x):
    hi = x.astype(BF16)
    r1 = x - hi.astype(F32)
    mid = r1.astype(BF16)
    lo = (r1 - mid.astype(F32)).astype(BF16)
    return hi, mid, lo


def _exact_lhs_dot(a01, x, dims=_NN):
    a = a01.astype(BF16)
    hi, mid, lo = _split3(x)
    d = lambda y: lax.dot_general(a, y, dims, preferred_element_type=F32)
    return d(hi) + (d(mid) + d(lo))


def _exact_rhs_dot(x, b01, dims=_NN):
    b = b01.astype(BF16)
    hi, mid, lo = _split3(x)
    d = lambda y: lax.dot_general(y, b, dims, preferred_element_type=F32)
    return d(hi) + (d(mid) + d(lo))


def _chunk_cumsum(x, C):
    R, W = x.shape
    SL = max(C, 64)
    rr = _iota((SL, SL), 0)
    cc = _iota((SL, SL), 1)
    tri = ((rr // C == cc // C) & (rr >= cc)).astype(F32)
    same_chunk = (rr // C == cc // C).astype(F32)
    is_last = (_iota((SL, 1), 0) % C) == (C - 1)
    cums, ends = [], []
    for s0 in range(0, R, SL):
        cum_s = _exact_lhs_dot(tri, x[s0:s0 + SL])
        cums.append(cum_s)
        if C == SL:
            ends.append(jnp.broadcast_to(cum_s[C - 1:C], (SL, W)))
        else:
            ends.append(_exact_lhs_dot(same_chunk, jnp.where(is_last, cum_s, 0.0)))
    cum = jnp.concatenate(cums, axis=0) if len(cums) > 1 else cums[0]
    cum_end = jnp.concatenate(ends, axis=0) if len(ends) > 1 else ends[0]
    return cum, cum_end


def _pair_stack(x, lane_a):
    return jnp.concatenate([jnp.where(lane_a, x, 0.0), jnp.where(lane_a, 0.0, x)], axis=0)


def _pair_stack_cat(x, C):
    first = _iota((1, 2 * C), 1) < C
    return jnp.concatenate([jnp.where(first, x, 0.0), jnp.where(first, 0.0, x)], axis=0)


def _load_rows(ref, NB, TB, decode):
    x = ref[...]
    if not decode:
        return x
    w = x.shape[-1]
    t_loc = _iota((NB * TB, 1), 0) % TB
    return jnp.where(t_loc == 0, jnp.broadcast_to(x[:, None, :], (NB, TB, w)).reshape(NB * TB, w), 0.0)


def _store_rows(ref, val, NB, TB, decode):
    ref[...] = val.reshape(NB, TB, val.shape[-1])[:, 0, :] if decode else val


def _row_spec(width, NB, TB, n_blk, decode, row0):
    if decode:
        return pl.BlockSpec((NB, width), lambda b, c: (row0 // NB + b, 0))
    return pl.BlockSpec((TB, width), lambda b, c: (b * n_blk + c, 0))


def _rwkv_kernel(*refs, C, NB, NC, t_valid, decode, n_alias):
    (p_ref, prev_ref, s0_ref, mu_ref, w0_ref, wa2_ref, a0_ref, g2_ref, kk_ref, ka_ref, rk_ref,
     lnw_ref, lnb_ref, ones_ref) = refs[:14]
    o_ref, shift_ref, sout_ref, s_scr, prev_scr, y_scr = refs[14 + n_alias:]
    blk = pl.program_id(1)
    n_blk = pl.num_programs(1)
    H = RW_HEAD
    TB = NC * C
    R = NB * TB

    @pl.when(blk == 0)
    def _init():
        prev_scr[...] = prev_ref[:, 0, :]
        s_scr[...] = jnp.zeros_like(s_scr)
        for nb in range(NB):
            for j in range(RW_PAIRS):
                s_scr[nb, j, 0:H, 0:H] = s0_ref[nb, 2 * j]
                s_scr[nb, j, H:2 * H, H:2 * H] = s0_ref[nb, 2 * j + 1]

    p = _load_rows(p_ref, NB, TB, decode)
    t_loc = _iota((R, 1), 0) % TB
    carried = jnp.broadcast_to(prev_scr[...][:, None, :], (NB, TB, RW_COLS)).reshape(R, RW_COLS)
    prev = jnp.where(t_loc == 0, carried, pltpu.roll(p, 1, axis=0))
    last_row = (t_valid - 1) % TB
    p_last = p.reshape(NB, TB, RW_COLS)[:, last_row, :]
    prev_scr[...] = p_last
    xs = p + (prev - p) * mu_ref[...]
    r = xs[:, 0:1024]
    k = xs[:, 1024:2048]
    v = xs[:, 2048:3072]
    z = xs[:, 3072:3200]
    xg = xs[:, 3200:3328]
    lane128 = _iota((1, LANES), 1)
    tz = jnp.where(lane128 < RW_HEAD, jnp.tanh(z), z)
    wa = _bdot(tz, wa2_ref[...])
    w_lin = w0_ref[...] + wa[:, 0:1024]
    log_w = -jnp.exp(-_softplus(-w_lin) - 0.5)
    a = _sigmoid(a0_ref[...] + wa[:, 1024:2048])
    g = _bdot(_sigmoid(xg), g2_ref[...])
    ones_bd = ones_ref[...]

    def headsum(x):
        return jnp.concatenate(
            [_exact_rhs_dot(x[:, LANES * j:LANES * (j + 1)], ones_bd) for j in range(RW_PAIRS)], axis=1)

    kk = k * kk_ref[...]
    k_hat = k * (1.0 + (a - 1.0) * ka_ref[...])
    sums = headsum(jnp.concatenate([kk * kk, r * k_hat * rk_ref[...]], axis=0))
    kk = kk * lax.rsqrt(jnp.maximum(sums[0:R], 1e-24))
    bonus = sums[R:2 * R] * v
    b = kk * a
    if t_valid % TB != 0:
        valid = (blk * TB + t_loc) < t_valid
        log_w = jnp.where(valid, log_w, 0.0)
        kk = jnp.where(valid, kk, 0.0)
        b = jnp.where(valid, b, 0.0)
        k_hat = jnp.where(valid, k_hat, 0.0)
    cum, cum_end = _chunk_cumsum(log_w, C)
    kap_t = kk * jnp.exp(cum - log_w)
    r_t = r * jnp.exp(cum)
    inv = jnp.exp(-cum)
    b_t = b * inv
    k_t = k_hat * inv
    to_end = jnp.exp(cum_end - cum)
    b_e = b * to_end
    k_e = k_hat * to_end
    p_end = jnp.exp(cum_end)

    lane_a = lane128 < RW_HEAD
    tt = _iota((C, 2 * C), 0)
    ss = _iota((C, 2 * C), 1) % C
    strict = tt > ss
    lower = tt >= ss
    eye_cat = (tt == ss).astype(F32)
    blockdiag = (_iota((LANES, LANES), 0) < RW_HEAD) == (_iota((LANES, LANES), 1) < RW_HEAD)

    units = [(nb, ci, j) for nb in range(NB) for ci in range(NC) for j in range(RW_PAIRS)]

    def rows_of(nb, ci):
        r0 = (nb * NC + ci) * C
        return slice(r0, r0 + C)

    def lanes_of(j):
        return slice(LANES * j, LANES * (j + 1))

    kr, a1, a2, a3, a4, tm = {}, {}, {}, {}, {}, {}
    for un in units:
        rs, sl = rows_of(un[0], un[1]), lanes_of(un[2])
        kr[un] = jnp.concatenate([kap_t[rs, sl], r_t[rs, sl]], axis=0)
    for un in units:
        rs, sl = rows_of(un[0], un[1]), lanes_of(un[2])
        g1 = _bdot(kr[un], _pair_stack(b_t[rs, sl], lane_a), _NT)
        a1[un] = jnp.where(strict, g1[0:C], 0.0)
        a3[un] = jnp.where(lower, g1[C:2 * C], 0.0)
    for un in units:
        rs, sl = rows_of(un[0], un[1]), lanes_of(un[2])
        g2 = _bdot(kr[un], _pair_stack(k_t[rs, sl], lane_a), _NT)
        a2[un] = jnp.where(strict, g2[0:C], 0.0)
        a4[un] = jnp.where(lower, g2[C:2 * C], 0.0)
    for un in units:
        tm[un] = eye_cat - jnp.where((tt // 2 == ss // 2), a1[un], 0.0)
    m = 2
    while m < C:
        sel = (tt // (2 * m) == ss // (2 * m)) & ((tt // m) % 2 == 1) & ((ss // m) % 2 == 0)
        inner = {}
        for un in units:
            inner[un] = _bdot(jnp.where(sel, a1[un], 0.0), _pair_stack_cat(tm[un], C))
        for un in units:
            tm[un] = tm[un] - _bdot(tm[un], _pair_stack_cat(inner[un], C))
        m *= 2
    av = {}
    for un in units:
        rs, sl = rows_of(un[0], un[1]), lanes_of(un[2])
        av[un] = _bdot(jnp.concatenate([a2[un], a4[un]], axis=0), _pair_stack(v[rs, sl], lane_a))
    for ci in range(NC):
        cur = [(nb, ci, j) for nb in range(NB) for j in range(RW_PAIRS)]
        xs_s, u = {}, {}
        for un in cur:
            xs_s[un] = _bdot(kr[un], s_scr[un[0], un[2]], _NT)
        for un in cur:
            u[un] = -_bdot(tm[un], _pair_stack(xs_s[un][0:C] + av[un][0:C], lane_a))
        for un in cur:
            rs, sl = rows_of(un[0], un[1]), lanes_of(un[2])
            ds = _bdot(jnp.concatenate([u[un], v[rs, sl]], axis=0),
                       jnp.concatenate([b_e[rs, sl], k_e[rs, sl]], axis=0), _TN)
            s_scr[un[0], un[2]] = (s_scr[un[0], un[2]] * p_end[rs.start:rs.start + 1, sl]
                                   + jnp.where(blockdiag, ds, 0.0))
        for un in cur:
            rs, sl = rows_of(un[0], un[1]), lanes_of(un[2])
            y_scr[rs, sl] = (xs_s[un][C:2 * C] + av[un][C:2 * C]
                             + _bdot(a3[un], _pair_stack(u[un], lane_a)))

    y = y_scr[...]
    mean = headsum(y) * (1.0 / RW_HEAD)
    yc = y - mean
    var = headsum(yc * yc) * (1.0 / RW_HEAD)
    yn = yc * lax.rsqrt(var + RW_GN_EPS) * lnw_ref[...] + lnb_ref[...]
    _store_rows(o_ref, (yn + bonus) * g, NB, TB, decode)

    @pl.when(blk == n_blk - 1)
    def _fin():
        shift_ref[:, 0, :] = p_last
        for nb in range(NB):
            for j in range(RW_PAIRS):
                sout_ref[nb, 2 * j] = s_scr[nb, j, 0:H, 0:H]
                sout_ref[nb, 2 * j + 1] = s_scr[nb, j, H:2 * H, H:2 * H]


class _Group:
    def __init__(self, batch, steps, chunk, nb, nc, decode, row0, layer, n_layers):
        self.B, self.T, self.C, self.NB, self.NC = batch, steps, chunk, nb, nc
        self.decode, self.row0, self.layer, self.n_layers = decode, row0, layer, n_layers
        self.TB = nc * chunk
        assert batch % nb == 0 and (decode or steps % self.TB == 0)
        self.n_blk = 1 if decode else steps // self.TB
        self.grid = (batch // nb, self.n_blk)

    def rows(self, width):
        return _row_spec(width, self.NB, self.TB, self.n_blk, self.decode, self.row0)

    def state(self, *dims):
        zeros = (0,) * len(dims)
        return pl.BlockSpec((None, self.NB) + dims, lambda b, c, l=self.layer: (l, b) + zeros)

    def state_shape(self, *dims):
        return jax.ShapeDtypeStruct((self.n_layers, self.B) + dims, F32)


def _alias_args(extra):
    return [pl.BlockSpec(memory_space=pl.ANY) for _ in extra]


def rwkv_branch(p, shift_prev, s0, prm, grp, n_tok, o_buf=None, s_buf=None):
    row = lambda w: pl.BlockSpec((1, w), lambda b, c: (0, 0))
    full = lambda a: pl.BlockSpec(a.shape, lambda b, c: (0,) * a.ndim)
    extra = [a for a in (o_buf, s_buf) if a is not None]
    n_in = 14
    aliases = {}
    if o_buf is not None:
        aliases[n_in] = 0
    if s_buf is not None:
        aliases[n_in + len(extra) - 1] = 2
    kern = functools.partial(_rwkv_kernel, C=grp.C, NB=grp.NB, NC=grp.NC, t_valid=grp.T, decode=grp.decode,
                             n_alias=len(extra))
    return pl.pallas_call(
        kern,
        grid=grp.grid,
        in_specs=[
            grp.rows(RW_COLS),
            grp.state(1, RW_COLS),
            grp.state(RW_HEADS, RW_HEAD, RW_HEAD),
            row(RW_COLS), row(1024), full(prm["wa2"]), row(1024), full(prm["g2"]),
            row(1024), row(1024), row(1024), row(1024), row(1024), full(prm["ones_bd"]),
        ] + _alias_args(extra),
        out_specs=[
            grp.rows(BRANCH_W),
            pl.BlockSpec((grp.NB, 1, RW_COLS), lambda b, c: (b, 0, 0)),
            grp.state(RW_HEADS, RW_HEAD, RW_HEAD),
        ],
        out_shape=[
            jax.ShapeDtypeStruct((n_tok, BRANCH_W), F32),
            jax.ShapeDtypeStruct((grp.B, 1, RW_COLS), F32),
            grp.state_shape(RW_HEADS, RW_HEAD, RW_HEAD),
        ],
        scratch_shapes=[
            pltpu.VMEM((grp.NB, RW_PAIRS, LANES, LANES), F32),
            pltpu.VMEM((grp.NB, RW_COLS), F32),
            pltpu.VMEM((grp.NB * grp.TB, BRANCH_W), F32),
        ],
        input_output_aliases=aliases,
        compiler_params=_params("parallel", "arbitrary"),
        name="rwkv7",
    )(p, shift_prev, s0, prm["mu"], prm["w0"], prm["wa2"], prm["a0"], prm["g2"], prm["kk"], prm["ka"],
      prm["rk"], prm["lnw"], prm["lnb"], prm["ones_bd"], *extra)


def _layernorm_lanes(x, w, b, eps):
    mean = jnp.mean(x, axis=-1, keepdims=True)
    xc = x - mean
    var = jnp.mean(xc * xc, axis=-1, keepdims=True)
    return xc * lax.rsqrt(var + eps) * w + b


def _gla_kernel(*refs, C, NB, NC, t_valid, decode, n_alias):
    pm_ref, pxg_ref, s0_ref, gw2_ref, gb_ref, lnw_ref, lnb_ref = refs[:7]
    o_ref, sout_ref, s_scr, o_scr = refs[7 + n_alias:]
    blk = pl.program_id(1)
    n_blk = pl.num_programs(1)
    TB = NC * C
    R = NB * TB

    @pl.when(blk == 0)
    def _init():
        s_scr[...] = s0_ref[...]

    pm = _load_rows(pm_ref, NB, TB, decode)
    q = pm[:, 0:GLA_DK] * (GLA_HK ** -0.5)
    k = pm[:, GLA_DK:2 * GLA_DK]
    v = pm[:, 1024:2048]
    r = pm[:, 2048:3072]
    zl = _bdot(_load_rows(pxg_ref, NB, TB, decode), gw2_ref[...]) + gb_ref[...]
    log_a = (jnp.minimum(zl, 0.0) - jnp.log(1.0 + jnp.exp(-jnp.abs(zl)))) * (1.0 / GLA_TAU)
    if t_valid % TB != 0:
        t_loc = _iota((R, 1), 0) % TB
        log_a = jnp.where((blk * TB + t_loc) < t_valid, log_a, 0.0)
    cum, cum_end = _chunk_cumsum(log_a, C)
    qe = q * jnp.exp(cum)
    ke = k * jnp.exp(-cum)
    kd = k * jnp.exp(cum_end - cum)
    decay = jnp.exp(cum_end)
    silu_r = r * _sigmoid(r)
    tri = _iota((C, C), 0) >= _iota((C, C), 1)

    units = [(nb, ci, h) for nb in range(NB) for ci in range(NC) for h in range(GLA_HEADS)]
    rows_of = lambda nb, ci: slice((nb * NC + ci) * C, (nb * NC + ci + 1) * C)
    kl_of = lambda h: slice(GLA_HK * h, GLA_HK * (h + 1))
    vl_of = lambda h: slice(GLA_HV * h, GLA_HV * (h + 1))
    a_mat, o_in, ds = {}, {}, {}
    for un in units:
        rs, kl = rows_of(un[0], un[1]), kl_of(un[2])
        a_mat[un] = jnp.where(tri, _bdot(qe[rs, kl], ke[rs, kl], _NT), 0.0)
    for un in units:
        rs, vl = rows_of(un[0], un[1]), vl_of(un[2])
        o_in[un] = _bdot(a_mat[un], v[rs, vl])
    for un in units:
        rs, kl, vl = rows_of(un[0], un[1]), kl_of(un[2]), vl_of(un[2])
        ds[un] = _bdot(kd[rs, kl], v[rs, vl], _TN)
    for ci in range(NC):
        cur = [(nb, ci, h) for nb in range(NB) for h in range(GLA_HEADS)]
        o = {}
        for un in cur:
            rs, kl = rows_of(un[0], un[1]), kl_of(un[2])
            o[un] = o_in[un] + _bdot(qe[rs, kl], s_scr[un[0], un[2]])
        for un in cur:
            rs, kl = rows_of(un[0], un[1]), kl_of(un[2])
            dcol = jnp.transpose(jnp.broadcast_to(decay[rs.start:rs.start + 1, kl], (GLA_HK, GLA_HK)))
            s_scr[un[0], un[2]] = s_scr[un[0], un[2]] * jnp.concatenate([dcol, dcol], axis=1) + ds[un]
        for un in cur:
            rs, vl = rows_of(un[0], un[1]), vl_of(un[2])
            o_scr[rs, vl] = _layernorm_lanes(o[un], lnw_ref[:, vl], lnb_ref[:, vl], LN_EPS) * silu_r[rs, vl]
    _store_rows(o_ref, o_scr[...], NB, TB, decode)

    @pl.when(blk == n_blk - 1)
    def _fin():
        sout_ref[...] = s_scr[...]


def gla_branch(pm, pxg, s0, prm, grp, n_tok, o_buf=None, s_buf=None):
    row = lambda w: pl.BlockSpec((1, w), lambda b, c: (0, 0))
    extra = [a for a in (o_buf, s_buf) if a is not None]
    n_in = 7
    aliases = {}
    if o_buf is not None:
        aliases[n_in] = 0
    if s_buf is not None:
        aliases[n_in + len(extra) - 1] = 1
    kern = functools.partial(_gla_kernel, C=grp.C, NB=grp.NB, NC=grp.NC, t_valid=grp.T, decode=grp.decode,
                             n_alias=len(extra))
    return pl.pallas_call(
        kern,
        grid=grp.grid,
        in_specs=[
            grp.rows(3 * BRANCH_W),
            grp.rows(LANES),
            grp.state(GLA_HEADS, GLA_HK, GLA_HV),
            pl.BlockSpec((LANES, GLA_DK), lambda b, c: (0, 0)),
            row(GLA_DK), row(BRANCH_W), row(BRANCH_W),
        ] + _alias_args(extra),
        out_specs=[
            grp.rows(BRANCH_W),
            grp.state(GLA_HEADS, GLA_HK, GLA_HV),
        ],
        out_shape=[
            jax.ShapeDtypeStruct((n_tok, BRANCH_W), F32),
            grp.state_shape(GLA_HEADS, GLA_HK, GLA_HV),
        ],
        scratch_shapes=[pltpu.VMEM((grp.NB, GLA_HEADS, GLA_HK, GLA_HV), F32),
                        pltpu.VMEM((grp.NB * grp.TB, BRANCH_W), F32)],
        input_output_aliases=aliases,
        compiler_params=_params("parallel", "arbitrary"),
        name="gla",
    )(pm, pxg, s0, prm["gw2"], prm["gb"], prm["lnw"], prm["lnb"], *extra)


def _cmlp_kernel(*refs, C, NB, decode, n_alias):
    p_ref, lnw_ref, lnb_ref, ws_ref, bt_ref = refs[:5]
    o_ref, v_ref, o_scr = refs[5 + n_alias:]
    c = pl.program_id(1)
    n_chunks = pl.num_programs(1)
    z = _gelu(_load_rows(p_ref, NB, C, decode))
    u = z[:, 0:CM_W]
    vn = _layernorm_lanes(z[:, CM_W:2 * CM_W], lnw_ref[...], lnb_ref[...], LN_EPS)
    tri = _iota((C, C), 0) >= _iota((C, C), 1)
    for g in range(CM_GROUPS):
        gl = slice(CM_GW * g, CM_GW * (g + 1))
        ws_c = jnp.where(tri, ws_ref[g], 0.0)
        for nb in range(NB):
            rs = slice(nb * C, (nb + 1) * C)
            mixed = _bdot(ws_c, vn[rs, gl]) + bt_ref[:, g:g + 1]
            o_scr[rs, gl] = u[rs, gl] * mixed
    _store_rows(o_ref, o_scr[...], NB, C, decode)

    @pl.when(c == n_chunks - 1)
    def _fin():
        v_ref[...] = vn.reshape(NB, C, CM_W)


def cmlp_branch(p, prm, grp, n_tok, o_buf=None):
    row = lambda w: pl.BlockSpec((1, w), lambda b, c: (0, 0))
    extra = [a for a in (o_buf,) if a is not None]
    C = grp.TB
    return pl.pallas_call(
        functools.partial(_cmlp_kernel, C=C, NB=grp.NB, decode=grp.decode, n_alias=len(extra)),
        grid=grp.grid,
        in_specs=[
            grp.rows(2 * CM_W),
            row(CM_W), row(CM_W),
            pl.BlockSpec((CM_GROUPS, C, C), lambda b, c: (0, 0, 0)),
            pl.BlockSpec((C, CM_GROUPS), lambda b, c: (0, 0)),
        ] + _alias_args(extra),
        out_specs=[
            grp.rows(CM_W),
            pl.BlockSpec((grp.NB, C, CM_W), lambda b, c: (b, 0, 0)),
        ],
        out_shape=[
            jax.ShapeDtypeStruct((n_tok, CM_W), F32),
            jax.ShapeDtypeStruct((grp.B, C, CM_W), F32),
        ],
        scratch_shapes=[pltpu.VMEM((grp.NB * C, CM_W), F32)],
        input_output_aliases={5: 0} if extra else {},
        compiler_params=_params("parallel", "arbitrary"),
        name="cmlp",
    )(p, prm["lnw"], prm["lnb"], prm["ws"], prm["bt"], *extra)


def _mix_kernel(orw_ref, ogl_ref, ocm_ref, g0_ref, g1_ref, g2_ref, wb_ref, o_ref):
    acc = _sigmoid(g0_ref[...]) * _bdot(orw_ref[...], wb_ref[0])
    acc += _sigmoid(g1_ref[...]) * _bdot(ogl_ref[...], wb_ref[1])
    acc += _sigmoid(g2_ref[...]) * _bdot(ocm_ref[...], wb_ref[2])
    o_ref[...] = acc.astype(o_ref.dtype)


def branch_mix(o_rw, o_gla, o_cm, p_gate, wb, tm, tn):
    n = o_rw.shape[0]
    nj = D_MODEL // tn
    ospec = pl.BlockSpec((tm, BRANCH_W), lambda i, j: (i, 0))
    gspec = lambda g: pl.BlockSpec((tm, tn), lambda i, j: (i, g * nj + j))
    return pl.pallas_call(
        _mix_kernel,
        grid=(n // tm, nj),
        in_specs=[ospec, ospec, ospec, gspec(0), gspec(1), gspec(2),
                  pl.BlockSpec((3, BRANCH_W, tn), lambda i, j: (0, 0, j))],
        out_specs=pl.BlockSpec((tm, tn), lambda i, j: (i, j)),
        out_shape=jax.ShapeDtypeStruct((n, D_MODEL), BF16),
        compiler_params=_params("parallel", "parallel"),
        name="branch_mix",
    )(o_rw, o_gla, o_cm, p_gate, p_gate, p_gate, wb)


def _top_rows(s, n):
    out = []
    for _ in range(n):
        m = jnp.max(s, axis=0, keepdims=True)
        out.append(m)
        s = jnp.where(s == m, NEG_INF, s)
    return out


def _router_kernel(q_ref, keys_ref, beta_ref, thr_ref, c_ref):
    tn = q_ref.shape[0]
    n_top = PEER_TOPK + 1
    for h in range(PEER_HEADS):
        s1 = _hdot(keys_ref[2 * h], q_ref[:, LANES * (2 * h):LANES * (2 * h + 1)], _NT)
        s2 = _hdot(keys_ref[2 * h + 1], q_ref[:, LANES * (2 * h + 1):LANES * (2 * h + 2)], _NT)
        a = _top_rows(s1, n_top)
        b = _top_rows(s2, n_top)
        pad = jnp.full((3 * SUBLANES - n_top, tn), NEG_INF, F32)
        b_all = jnp.concatenate(b + [pad], axis=0)
        cand = jnp.concatenate([a[0] + b_all] + [a[x] + b_all[0:SUBLANES] for x in range(1, n_top)], axis=0)
        best = _top_rows(cand, n_top)
        top = best[0]
        zsum = jnp.zeros_like(top)
        for t in range(PEER_TOPK):
            zsum += jnp.exp(best[t] - top)
        tau = 0.5 * (best[PEER_TOPK - 1] + best[PEER_TOPK])
        rows = slice(SUBLANES * h, SUBLANES * (h + 1))
        beta_ref[h] = s2 - b[0]
        thr_ref[:, rows, :] = ((tau - top) - (s1 - a[0])).reshape(PEER_ITILES, SUBLANES, tn)
        c_ref[:, rows, :] = (jnp.exp(s1 - a[0]) / zsum).reshape(PEER_ITILES, SUBLANES, tn)


def peer_router(q, keys, tn):
    n = q.shape[0]
    spec = pl.BlockSpec((PEER_HEADS, PEER_NKEYS, tn), lambda i: (0, 0, i))
    shp = jax.ShapeDtypeStruct((PEER_HEADS, PEER_NKEYS, n), F32)
    spec_i = pl.BlockSpec((PEER_ITILES, PEER_HEADS * SUBLANES, tn), lambda i: (0, 0, i))
    shp_i = jax.ShapeDtypeStruct((PEER_ITILES, PEER_HEADS * SUBLANES, n), F32)
    return pl.pallas_call(
        _router_kernel,
        grid=(n // tn,),
        in_specs=[pl.BlockSpec((tn, D_MODEL), lambda i: (i, 0)),
                  pl.BlockSpec((2 * PEER_HEADS, PEER_NKEYS, PEER_DH), lambda i: (0, 0, 0))],
        out_specs=[spec, spec_i, spec_i],
        out_shape=[shp, shp_i, shp_i],
        compiler_params=_params("parallel"),
        name="peer_router",
    )(q, keys)


def _peer_kernel(xn_ref, u_ref, v_ref, beta_ref, thr_ref, c_ref, res_ref, o_ref, eb_scr, *, ti):
    e = pl.program_id(1)
    sub_i = PEER_SUB_KEYS

    @pl.when(e == 0)
    def _init():
        o_ref[...] = res_ref[...]
        eb_scr[...] = jnp.exp(beta_ref[...])

    sub = (e % (SUBLANES // ti)) * ti
    xn = xn_ref[...]
    starts = list(range(0, ti, sub_i))

    def scores(s0):
        rows = slice(s0 * PEER_NKEYS, (s0 + sub_i) * PEER_NKEYS)
        return lax.dot_general(u_ref[rows, :], xn, _NT, preferred_element_type=F32)

    def weights(s0, ht):
        gates = []
        for ii in range(s0, s0 + sub_i):
            acc = None
            for h in range(PEER_HEADS):
                r = pl.ds(SUBLANES * h + sub + ii, 1)
                hit = beta_ref[h] >= thr_ref[0, r, :]
                term = jnp.where(hit, eb_scr[h] * c_ref[0, r, :], 0.0)
                acc = term if acc is None else acc + term
            gates.append(acc)
        g = jnp.concatenate(gates, axis=0) if len(gates) > 1 else gates[0]
        return (g * _gelu(ht)).astype(BF16)

    tot = None
    ht_next = scores(starts[0])
    for n, s0 in enumerate(starts):
        ht = ht_next
        if n + 1 < len(starts):
            ht_next = scores(starts[n + 1])
        rows = slice(s0 * PEER_NKEYS, (s0 + sub_i) * PEER_NKEYS)
        part = lax.dot_general(weights(s0, ht), v_ref[rows, :], _TN, preferred_element_type=F32)
        tot = part if tot is None else tot + part
    o_ref[...] += tot


def peer_experts(xn, u, v, layer, beta, thr, cc, res, tm, ti):
    n = xn.shape[0]
    te = ti * PEER_NKEYS
    per = SUBLANES // ti
    once = dict(pipeline_mode=pl.Buffered(1))
    ispec = pl.BlockSpec((1, PEER_HEADS * SUBLANES, tm), lambda i, e: (e // per, 0, i))
    return pl.pallas_call(
        functools.partial(_peer_kernel, ti=ti),
        grid=(n // tm, PEER_NKEYS // ti),
        in_specs=[
            pl.BlockSpec((tm, D_MODEL), lambda i, e: (i, 0), **once),
            pl.BlockSpec((None, te, D_MODEL), lambda i, e: (layer, e, 0)),
            pl.BlockSpec((None, te, D_MODEL), lambda i, e: (layer, e, 0)),
            pl.BlockSpec((PEER_HEADS, PEER_NKEYS, tm), lambda i, e: (0, 0, i), **once),
            ispec, ispec,
            pl.BlockSpec((tm, D_MODEL), lambda i, e: (i, 0), **once),
        ],
        out_specs=pl.BlockSpec((tm, D_MODEL), lambda i, e: (i, 0)),
        out_shape=jax.ShapeDtypeStruct((n, D_MODEL), F32),
        scratch_shapes=[pltpu.VMEM((PEER_HEADS, PEER_NKEYS, tm), F32)],
        compiler_params=_params("parallel", "arbitrary"),
        name="peer_experts",
    )(xn, u, v, beta, thr, cc, res)


TOKEN_TILE = 640
PROMPT_CHUNKS_PER_STEP = 4
DECODE_ROWS_PER_STEP = 8
PEER_KEYS_PER_STEP = 8
W_IN_COLS = dict(rw=(0, 3328), gla_qkv=(3328, 5376), gla_xg=(5376, 5392), gla_r=(5392, 6416),
                 cm=(6416, 8464), gate=(8464, 14608))


def _proj(hn, w, tn):
    return matmul(hn, w.astype(BF16), TOKEN_TILE, tn)


def kernel(x_prompt, x_sample, state_rw_shift, state_rwkv, state_gla, norm1, w_in, rw_mu, rw_w0, rw_w2, rw_a0, rw_a2, rw_g2, rw_kk, rw_ka, rw_rk, rw_lnw, rw_lnb, gla_gw2, gla_gb, gla_lnw, gla_lnb, cm_lnw, cm_lnb, cm_ws, cm_b, w_branch, w_out, norm2, peer_wq, peer_keys, peer_u, peer_v, norm_f):
    bp, tp, d = x_prompt.shape
    bs = x_sample.shape[0]
    n_p = bp * tp
    n_tok = n_p + bs
    x = jnp.concatenate([x_prompt.reshape(n_p, d), x_sample.reshape(bs, d)], axis=0)
    head_of_lane = jnp.arange(LANES) // RW_HEAD
    ones_bd = (head_of_lane[:, None] == head_of_lane[None, :]).astype(F32)
    row = lambda a: a.reshape(1, -1)
    dec_steps = SUBLANES
    cm_chunk_p = min(CM_CHUNK, tp)
    peer_u_bf = peer_u.astype(BF16)
    peer_v_bf = peer_v.astype(BF16)
    zero_shift = jnp.zeros((1, bp, 1, RW_COLS), F32)
    zero_rw = jnp.zeros((1, bp, RW_HEADS, RW_HEAD, RW_HEAD), F32)
    zero_gla = jnp.zeros((1, bp, GLA_HEADS, GLA_HK, GLA_HV), F32)
    shift_in = state_rw_shift[:, :, None, :]
    outs = {k: [] for k in ("p_sh", "p_rw", "p_gla", "p_cm", "s_sh", "s_cm")}
    s_rw = s_gla = None
    for l in range(DEPTH):
        prompt = lambda chunk, nc: _Group(bp, tp, chunk, 1, nc, False, 0, 0, 1)
        sample = _Group(bs, 1, dec_steps, DECODE_ROWS_PER_STEP, 1, True, n_p, l, DEPTH)
        wl = w_in[l]
        cols = lambda name: wl[:, W_IN_COLS[name][0]:W_IN_COLS[name][1]]
        hn = rmsnorm(x, norm1[l], BF16, TOKEN_TILE)
        p_rw = _proj(hn, cols("rw"), 1664)
        p_gl = _proj(hn, jnp.concatenate([cols("gla_qkv"), cols("gla_r")], axis=1), 1536)
        p_xg = _proj(hn, jnp.pad(cols("gla_xg"), ((0, 0), (0, LANES - GLA_GATE_LORA))), LANES)
        p_cm = _proj(hn, cols("cm"), 1024)
        p_gt = _proj(hn, cols("gate"), 1536)

        wa2 = jnp.zeros((LANES, 2 * BRANCH_W), F32)
        wa2 = wa2.at[0:RW_HEAD, 0:BRANCH_W].set(rw_w2[l]).at[RW_HEAD:LANES, BRANCH_W:].set(rw_a2[l])
        rw_prm = dict(mu=row(rw_mu[l]), w0=row(rw_w0[l]), wa2=wa2.astype(BF16), a0=row(rw_a0[l]),
                      g2=rw_g2[l].astype(BF16), kk=row(rw_kk[l]), ka=row(rw_ka[l]), rk=row(rw_rk[l]),
                      lnw=row(rw_lnw[l]), lnb=row(rw_lnb[l]), ones_bd=ones_bd)
        o_rw, sh_p, st_p = rwkv_branch(p_rw, zero_shift, zero_rw, rw_prm,
                                       prompt(RW_CHUNK, PROMPT_CHUNKS_PER_STEP), n_tok)
        o_rw, sh_s, s_rw = rwkv_branch(p_rw, shift_in, state_rwkv, rw_prm, sample, n_tok, o_buf=o_rw, s_buf=s_rw)
        outs["p_sh"].append(sh_p[:, 0])
        outs["p_rw"].append(st_p[0])
        outs["s_sh"].append(sh_s[:, 0])

        gla_prm = dict(gw2=jnp.pad(gla_gw2[l], ((0, LANES - GLA_GATE_LORA), (0, 0))).astype(BF16),
                       gb=row(gla_gb[l]), lnw=row(gla_lnw[l]), lnb=row(gla_lnb[l]))
        o_gla, st_p = gla_branch(p_gl, p_xg, zero_gla, gla_prm, prompt(GLA_CHUNK, PROMPT_CHUNKS_PER_STEP), n_tok)
        o_gla, s_gla = gla_branch(p_gl, p_xg, state_gla, gla_prm, sample, n_tok, o_buf=o_gla, s_buf=s_gla)
        outs["p_gla"].append(st_p[0])

        cm_prm = lambda c: dict(lnw=row(cm_lnw[l]), lnb=row(cm_lnb[l]), ws=cm_ws[l][:, :c, :c],
                                bt=jnp.transpose(cm_b[l])[:c])
        o_cm, v_p = cmlp_branch(p_cm, cm_prm(cm_chunk_p), prompt(cm_chunk_p, 1), n_tok)
        o_cm, v_s = cmlp_branch(p_cm, cm_prm(dec_steps), sample, n_tok, o_buf=o_cm)
        outs["p_cm"].append(v_p[:, :tp - ((tp - 1) // CM_CHUNK) * CM_CHUNK])
        outs["s_cm"].append(v_s[:, 0:1])

        mix = branch_mix(o_rw, o_gla, o_cm, p_gt, w_branch[l].astype(BF16), TOKEN_TILE, 512)
        x = matmul(mix, w_out[l].astype(BF16), TOKEN_TILE, 1024, res=x)

        hf = rmsnorm(x, norm2[l], BF16, TOKEN_TILE)
        q = matmul(hf, peer_wq[l].astype(BF16), TOKEN_TILE, 1024)
        beta, thr, cc = peer_router(q, peer_keys[l].reshape(2 * PEER_HEADS, PEER_NKEYS, PEER_DH), TOKEN_TILE)
        x = peer_experts(hf, peer_u_bf, peer_v_bf, l, beta, thr, cc, x, TOKEN_TILE, PEER_KEYS_PER_STEP)

    y = rmsnorm(x, norm_f, F32, TOKEN_TILE)
    st = lambda k: jnp.stack(outs[k])
    return (y[:n_p].reshape(bp, tp, d), y[n_p:].reshape(bs, 1, d), st("p_sh"), st("p_rw"), st("p_gla"), st("p_cm"),
            st("s_sh"), s_rw, s_gla, st("s_cm"))
```

```python
import functools
import math

import jax
import jax.numpy as jnp
from jax import lax
from jax.experimental import pallas as pl
from jax.experimental.pallas import tpu as pltpu

F32 = jnp.float32
BF16 = jnp.bfloat16

D_MODEL = 2048
DEPTH = 4
BRANCH_W = 1024
RW_HEAD = 64
RW_HEADS = 16
RW_PAIRS = RW_HEADS // 2
RW_COLS = 3328
RW_GN_EPS = 64e-5
RW_CHUNK = 64
GLA_HEADS = 4
GLA_HK = 128
GLA_HV = 256
GLA_DK = 512
GLA_GATE_LORA = 16
GLA_TAU = 16.0
GLA_CHUNK = 64
CM_W = 1024
CM_CHUNK = 128
CM_GROUPS = 8
CM_GW = 128
PEER_HEADS = 8
PEER_NKEYS = 128
PEER_N = PEER_NKEYS * PEER_NKEYS
PEER_DH = 128
PEER_TOPK = 16
PEER_ITILES = 16
PEER_SUB_KEYS = 2
RMS_EPS = 1e-6
LN_EPS = 1e-5

LANES = 128
SUBLANES = 8
VMEM_LIMIT_BYTES = 56 * 1024 * 1024

NEG_INF = float("-inf")

_NN = (((1,), (0,)), ((), ()))
_NT = (((1,), (1,)), ((), ()))
_TN = (((0,), (0,)), ((), ()))


def _params(*sem):
    return pltpu.CompilerParams(dimension_semantics=sem, vmem_limit_bytes=VMEM_LIMIT_BYTES)


def _bdot(a, b, dims=_NN):
    return lax.dot_general(a.astype(BF16), b.astype(BF16), dims, preferred_element_type=F32)


def _hdot(a, b, dims=_NN):
    return lax.dot_general(a.astype(F32), b.astype(F32), dims, preferred_element_type=F32,
                           precision=lax.Precision.HIGHEST)


def _gelu(x):
    return 0.5 * x * (1.0 + lax.erf(x * (1.0 / math.sqrt(2.0))))


def _sigmoid(x):
    return 1.0 / (1.0 + jnp.exp(-x))


def _softplus(x):
    return jnp.maximum(x, 0.0) + jnp.log(1.0 + jnp.exp(-jnp.abs(x)))


def _iota(shape, dim):
    return lax.broadcasted_iota(jnp.int32, shape, dim)


def _rmsnorm_kernel(x_ref, g_ref, o_ref):
    x = x_ref[...]
    ms = jnp.mean(x * x, axis=-1, keepdims=True)
    o_ref[...] = (x * lax.rsqrt(ms + RMS_EPS) * g_ref[...]).astype(o_ref.dtype)


def rmsnorm(x, g, out_dtype, tm):
    n, d = x.shape
    assert n % tm == 0
    return pl.pallas_call(
        _rmsnorm_kernel,
        grid=(n // tm,),
        in_specs=[pl.BlockSpec((tm, d), lambda i: (i, 0)), pl.BlockSpec((1, d), lambda i: (0, 0))],
        out_specs=pl.BlockSpec((tm, d), lambda i: (i, 0)),
        out_shape=jax.ShapeDtypeStruct((n, d), out_dtype),
        compiler_params=_params("parallel"),
        name="rmsnorm",
    )(x, g.reshape(1, d))


def _matmul_kernel(a_ref, w_ref, o_ref):
    o_ref[...] = jnp.dot(a_ref[...], w_ref[...], preferred_element_type=F32)


def _matmul_res_kernel(a_ref, w_ref, r_ref, o_ref):
    o_ref[...] = r_ref[...] + jnp.dot(a_ref[...], w_ref[...], preferred_element_type=F32)


def matmul(a, w, tm, tn, res=None):
    n, k = a.shape
    m = w.shape[1]
    assert n % tm == 0 and m % tn == 0
    in_specs = [pl.BlockSpec((tm, k), lambda j, i: (i, 0)), pl.BlockSpec((k, tn), lambda j, i: (0, j))]
    args = [a, w]
    body = _matmul_kernel
    if res is not None:
        in_specs.append(pl.BlockSpec((tm, tn), lambda j, i: (i, j)))
        args.append(res)
        body = _matmul_res_kernel
    return pl.pallas_call(
        body,
        grid=(m // tn, n // tm),
        in_specs=in_specs,
        out_specs=pl.BlockSpec((tm, tn), lambda j, i: (i, j)),
        out_shape=jax.ShapeDtypeStruct((n, m), F32),
        compiler_params=_params("parallel", "parallel"),
        name="matmul",
    )(*args)


def _split3(x):
    hi = x.astype(BF16)
    r1 = x - hi.astype(F32)
    mid = r1.astype(BF16)
    lo = (r1 - mid.astype(F32)).astype(BF16)
    return hi, mid, lo


def _exact_lhs_dot(a01, x, dims=_NN):
    a = a01.astype(BF16)
    hi, mid, lo = _split3(x)
    d = lambda y: lax.dot_general(a, y, dims, preferred_element_type=F32)
    return d(hi) + (d(mid) + d(lo))


def _exact_rhs_dot(x, b01, dims=_NN):
    b = b01.astype(BF16)
    hi, mid, lo = _split3(x)
    d = lambda y: lax.dot_general(y, b, dims, preferred_element_type=F32)
    return d(hi) + (d(mid) + d(lo))


def _chunk_cumsum(x, C):
    R, W = x.shape
    SL = max(C, 64)
    rr = _iota((SL, SL), 0)
    cc = _iota((SL, SL), 1)
    tri = ((rr // C == cc // C) & (rr >= cc)).astype(F32)
    same_chunk = (rr // C == cc // C).astype(F32)
    is_last = (_iota((SL, 1), 0) % C) == (C - 1)
    cums, ends = [], []
    for s0 in range(0, R, SL):
        cum_s = _exact_lhs_dot(tri, x[s0:s0 + SL])
        cums.append(cum_s)
        if C == SL:
            ends.append(jnp.broadcast_to(cum_s[C - 1:C], (SL, W)))
        else:
            ends.append(_exact_lhs_dot(same_chunk, jnp.where(is_last, cum_s, 0.0)))
    cum = jnp.concatenate(cums, axis=0) if len(cums) > 1 else cums[0]
    cum_end = jnp.concatenate(ends, axis=0) if len(ends) > 1 else ends[0]
    return cum, cum_end


def _pair_stack(x, lane_a):
    return jnp.concatenate([jnp.where(lane_a, x, 0.0), jnp.where(lane_a, 0.0, x)], axis=0)


def _pair_stack_cat(x, C):
    first = _iota((1, 2 * C), 1) < C
    return jnp.concatenate([jnp.where(first, x, 0.0), jnp.where(first, 0.0, x)], axis=0)


def _load_rows(ref, NB, TB, decode):
    x = ref[...]
    if not decode:
        return x
    w = x.shape[-1]
    t_loc = _iota((NB * TB, 1), 0) % TB
    return jnp.where(t_loc == 0, jnp.broadcast_to(x[:, None, :], (NB, TB, w)).reshape(NB * TB, w), 0.0)


def _store_rows(ref, val, NB, TB, decode):
    ref[...] = val.reshape(NB, TB, val.shape[-1])[:, 0, :] if decode else val


def _row_spec(width, NB, TB, n_blk, decode, row0):
    if decode:
        return pl.BlockSpec((NB, width), lambda b, c: (row0 // NB + b, 0))
    return pl.BlockSpec((TB, width), lambda b, c: (b * n_blk + c, 0))


def _rwkv_kernel(*refs, C, NB, NC, t_valid, decode, n_alias):
    (p_ref, prev_ref, s0_ref, mu_ref, w0_ref, wa2_ref, a0_ref, g2_ref, kk_ref, ka_ref, rk_ref,
     lnw_ref, lnb_ref, ones_ref) = refs[:14]
    o_ref, shift_ref, sout_ref, s_scr, prev_scr, y_scr = refs[14 + n_alias:]
    blk = pl.program_id(1)
    n_blk = pl.num_programs(1)
    H = RW_HEAD
    TB = NC * C
    R = NB * TB

    @pl.when(blk == 0)
    def _init():
        prev_scr[...] = prev_ref[:, 0, :]
        s_scr[...] = jnp.zeros_like(s_scr)
        for nb in range(NB):
            for j in range(RW_PAIRS):
                s_scr[nb, j, 0:H, 0:H] = s0_ref[nb, 2 * j]
                s_scr[nb, j, H:2 * H, H:2 * H] = s0_ref[nb, 2 * j + 1]

    p = _load_rows(p_ref, NB, TB, decode)
    t_loc = _iota((R, 1), 0) % TB
    carried = jnp.broadcast_to(prev_scr[...][:, None, :], (NB, TB, RW_COLS)).reshape(R, RW_COLS)
    prev = jnp.where(t_loc == 0, carried, pltpu.roll(p, 1, axis=0))
    last_row = (t_valid - 1) % TB
    p_last = p.reshape(NB, TB, RW_COLS)[:, last_row, :]
    prev_scr[...] = p_last
    xs = p + (prev - p) * mu_ref[...]
    r = xs[:, 0:1024]
    k = xs[:, 1024:2048]
    v = xs[:, 2048:3072]
    z = xs[:, 3072:3200]
    xg = xs[:, 3200:3328]
    lane128 = _iota((1, LANES), 1)
    tz = jnp.where(lane128 < RW_HEAD, jnp.tanh(z), z)
    wa = _bdot(tz, wa2_ref[...])
    w_lin = w0_ref[...] + wa[:, 0:1024]
    log_w = -jnp.exp(-_softplus(-w_lin) - 0.5)
    a = _sigmoid(a0_ref[...] + wa[:, 1024:2048])
    g = _bdot(_sigmoid(xg), g2_ref[...])
    ones_bd = ones_ref[...]

    def headsum(x):
        return jnp.concatenate(
            [_exact_rhs_dot(x[:, LANES * j:LANES * (j + 1)], ones_bd) for j in range(RW_PAIRS)], axis=1)

    kk = k * kk_ref[...]
    k_hat = k * (1.0 + (a - 1.0) * ka_ref[...])
    sums = headsum(jnp.concatenate([kk * kk, r * k_hat * rk_ref[...]], axis=0))
    kk = kk * lax.rsqrt(jnp.maximum(sums[0:R], 1e-24))
    bonus = sums[R:2 * R] * v
    b = kk * a
    if t_valid % TB != 0:
        valid = (blk * TB + t_loc) < t_valid
        log_w = jnp.where(valid, log_w, 0.0)
        kk = jnp.where(valid, kk, 0.0)
        b = jnp.where(valid, b, 0.0)
        k_hat = jnp.where(valid, k_hat, 0.0)
    cum, cum_end = _chunk_cumsum(log_w, C)
    kap_t = kk * jnp.exp(cum - log_w)
    r_t = r * jnp.exp(cum)
    inv = jnp.exp(-cum)
    b_t = b * inv
    k_t = k_hat * inv
    to_end = jnp.exp(cum_end - cum)
    b_e = b * to_end
    k_e = k_hat * to_end
    p_end = jnp.exp(cum_end)

    lane_a = lane128 < RW_HEAD
    tt = _iota((C, 2 * C), 0)
    ss = _iota((C, 2 * C), 1) % C
    strict = tt > ss
    lower = tt >= ss
    eye_cat = (tt == ss).astype(F32)
    blockdiag = (_iota((LANES, LANES), 0) < RW_HEAD) == (_iota((LANES, LANES), 1) < RW_HEAD)

    units = [(nb, ci, j) for nb in range(NB) for ci in range(NC) for j in range(RW_PAIRS)]

    def rows_of(nb, ci):
        r0 = (nb * NC + ci) * C
        return slice(r0, r0 + C)

    def lanes_of(j):
        return slice(LANES * j, LANES * (j + 1))

    kr, a1, a2, a3, a4, tm = {}, {}, {}, {}, {}, {}
    for un in units:
        rs, sl = rows_of(un[0], un[1]), lanes_of(un[2])
        kr[un] = jnp.concatenate([kap_t[rs, sl], r_t[rs, sl]], axis=0)
    for un in units:
        rs, sl = rows_of(un[0], un[1]), lanes_of(un[2])
        g1 = _bdot(kr[un], _pair_stack(b_t[rs, sl], lane_a), _NT)
        a1[un] = jnp.where(strict, g1[0:C], 0.0)
        a3[un] = jnp.where(lower, g1[C:2 * C], 0.0)
    for un in units:
        rs, sl = rows_of(un[0], un[1]), lanes_of(un[2])
        g2 = _bdot(kr[un], _pair_stack(k_t[rs, sl], lane_a), _NT)
        a2[un] = jnp.where(strict, g2[0:C], 0.0)
        a4[un] = jnp.where(lower, g2[C:2 * C], 0.0)
    for un in units:
        tm[un] = eye_cat - jnp.where((tt // 2 == ss // 2), a1[un], 0.0)
    m = 2
    while m < C:
        sel = (tt // (2 * m) == ss // (2 * m)) & ((tt // m) % 2 == 1) & ((ss // m) % 2 == 0)
        inner = {}
        for un in units:
            inner[un] = _bdot(jnp.where(sel, a1[un], 0.0), _pair_stack_cat(tm[un], C))
        for un in units:
            tm[un] = tm[un] - _bdot(tm[un], _pair_stack_cat(inner[un], C))
        m *= 2
    av = {}
    for un in units:
        rs, sl = rows_of(un[0], un[1]), lanes_of(un[2])
        av[un] = _bdot(jnp.concatenate([a2[un], a4[un]], axis=0), _pair_stack(v[rs, sl], lane_a))
    for ci in range(NC):
        cur = [(nb, ci, j) for nb in range(NB) for j in range(RW_PAIRS)]
        xs_s, u = {}, {}
        for un in cur:
            xs_s[un] = _bdot(kr[un], s_scr[un[0], un[2]], _NT)
        for un in cur:
            u[un] = -_bdot(tm[un], _pair_stack(xs_s[un][0:C] + av[un][0:C], lane_a))
        for un in cur:
            rs, sl = rows_of(un[0], un[1]), lanes_of(un[2])
            ds = _bdot(jnp.concatenate([u[un], v[rs, sl]], axis=0),
                       jnp.concatenate([b_e[rs, sl], k_e[rs, sl]], axis=0), _TN)
            s_scr[un[0], un[2]] = (s_scr[un[0], un[2]] * p_end[rs.start:rs.start + 1, sl]
                                   + jnp.where(blockdiag, ds, 0.0))
        for un in cur:
            rs, sl = rows_of(un[0], un[1]), lanes_of(un[2])
            y_scr[rs, sl] = (xs_s[un][C:2 * C] + av[un][C:2 * C]
                             + _bdot(a3[un], _pair_stack(u[un], lane_a)))

    y = y_scr[...]
    mean = headsum(y) * (1.0 / RW_HEAD)
    yc = y - mean
    var = headsum(yc * yc) * (1.0 / RW_HEAD)
    yn = yc * lax.rsqrt(var + RW_GN_EPS) * lnw_ref[...] + lnb_ref[...]
    _store_rows(o_ref, (yn + bonus) * g, NB, TB, decode)

    @pl.when(blk == n_blk - 1)
    def _fin():
        shift_ref[:, 0, :] = p_last
        for nb in range(NB):
            for j in range(RW_PAIRS):
                sout_ref[nb, 2 * j] = s_scr[nb, j, 0:H, 0:H]
                sout_ref[nb, 2 * j + 1] = s_scr[nb, j, H:2 * H, H:2 * H]


class _Group:
    def __init__(self, batch, steps, chunk, nb, nc, decode, row0, layer, n_layers):
        self.B, self.T, self.C, self.NB, self.NC = batch, steps, chunk, nb, nc
        self.decode, self.row0, self.layer, self.n_layers = decode, row0, layer, n_layers
        self.TB = nc * chunk
        assert batch % nb == 0 and (decode or steps % self.TB == 0)
        self.n_blk = 1 if decode else steps // self.TB
        self.grid = (batch // nb, self.n_blk)

    def rows(self, width):
        return _row_spec(width, self.NB, self.TB, self.n_blk, self.decode, self.row0)

    def state(self, *dims):
        zeros = (0,) * len(dims)
        return pl.BlockSpec((None, self.NB) + dims, lambda b, c, l=self.layer: (l, b) + zeros)

    def state_shape(self, *dims):
        return jax.ShapeDtypeStruct((self.n_layers, self.B) + dims, F32)


def _alias_args(extra):
    return [pl.BlockSpec(memory_space=pl.ANY) for _ in extra]


def rwkv_branch(p, shift_prev, s0, prm, grp, n_tok, o_buf=None, s_buf=None):
    row = lambda w: pl.BlockSpec((1, w), lambda b, c: (0, 0))
    full = lambda a: pl.BlockSpec(a.shape, lambda b, c: (0,) * a.ndim)
    extra = [a for a in (o_buf, s_buf) if a is not None]
    n_in = 14
    aliases = {}
    if o_buf is not None:
        aliases[n_in] = 0
    if s_buf is not None:
        aliases[n_in + len(extra) - 1] = 2
    kern = functools.partial(_rwkv_kernel, C=grp.C, NB=grp.NB, NC=grp.NC, t_valid=grp.T, decode=grp.decode,
                             n_alias=len(extra))
    return pl.pallas_call(
        kern,
        grid=grp.grid,
        in_specs=[
            grp.rows(RW_COLS),
            grp.state(1, RW_COLS),
            grp.state(RW_HEADS, RW_HEAD, RW_HEAD),
            row(RW_COLS), row(1024), full(prm["wa2"]), row(1024), full(prm["g2"]),
            row(1024), row(1024), row(1024), row(1024), row(1024), full(prm["ones_bd"]),
        ] + _alias_args(extra),
        out_specs=[
            grp.rows(BRANCH_W),
            pl.BlockSpec((grp.NB, 1, RW_COLS), lambda b, c: (b, 0, 0)),
            grp.state(RW_HEADS, RW_HEAD, RW_HEAD),
        ],
        out_shape=[
            jax.ShapeDtypeStruct((n_tok, BRANCH_W), F32),
            jax.ShapeDtypeStruct((grp.B, 1, RW_COLS), F32),
            grp.state_shape(RW_HEADS, RW_HEAD, RW_HEAD),
        ],
        scratch_shapes=[
            pltpu.VMEM((grp.NB, RW_PAIRS, LANES, LANES), F32),
            pltpu.VMEM((grp.NB, RW_COLS), F32),
            pltpu.VMEM((grp.NB * grp.TB, BRANCH_W), F32),
        ],
        input_output_aliases=aliases,
        compiler_params=_params("parallel", "arbitrary"),
        name="rwkv7",
    )(p, shift_prev, s0, prm["mu"], prm["w0"], prm["wa2"], prm["a0"], prm["g2"], prm["kk"], prm["ka"],
      prm["rk"], prm["lnw"], prm["lnb"], prm["ones_bd"], *extra)


def _layernorm_lanes(x, w, b, eps):
    mean = jnp.mean(x, axis=-1, keepdims=True)
    xc = x - mean
    var = jnp.mean(xc * xc, axis=-1, keepdims=True)
    return xc * lax.rsqrt(var + eps) * w + b


def _gla_kernel(*refs, C, NB, NC, t_valid, decode, n_alias):
    pm_ref, pxg_ref, s0_ref, gw2_ref, gb_ref, lnw_ref, lnb_ref = refs[:7]
    o_ref, sout_ref, s_scr, o_scr = refs[7 + n_alias:]
    blk = pl.program_id(1)
    n_blk = pl.num_programs(1)
    TB = NC * C
    R = NB * TB

    @pl.when(blk == 0)
    def _init():
        s_scr[...] = s0_ref[...]

    pm = _load_rows(pm_ref, NB, TB, decode)
    q = pm[:, 0:GLA_DK] * (GLA_HK ** -0.5)
    k = pm[:, GLA_DK:2 * GLA_DK]
    v = pm[:, 1024:2048]
    r = pm[:, 2048:3072]
    zl = _bdot(_load_rows(pxg_ref, NB, TB, decode), gw2_ref[...]) + gb_ref[...]
    log_a = (jnp.minimum(zl, 0.0) - jnp.log(1.0 + jnp.exp(-jnp.abs(zl)))) * (1.0 / GLA_TAU)
    if t_valid % TB != 0:
        t_loc = _iota((R, 1), 0) % TB
        log_a = jnp.where((blk * TB + t_loc) < t_valid, log_a, 0.0)
    cum, cum_end = _chunk_cumsum(log_a, C)
    qe = q * jnp.exp(cum)
    ke = k * jnp.exp(-cum)
    kd = k * jnp.exp(cum_end - cum)
    decay = jnp.exp(cum_end)
    silu_r = r * _sigmoid(r)
    tri = _iota((C, C), 0) >= _iota((C, C), 1)

    units = [(nb, ci, h) for nb in range(NB) for ci in range(NC) for h in range(GLA_HEADS)]
    rows_of = lambda nb, ci: slice((nb * NC + ci) * C, (nb * NC + ci + 1) * C)
    kl_of = lambda h: slice(GLA_HK * h, GLA_HK * (h + 1))
    vl_of = lambda h: slice(GLA_HV * h, GLA_HV * (h + 1))
    a_mat, o_in, ds = {}, {}, {}
    for un in units:
        rs, kl = rows_of(un[0], un[1]), kl_of(un[2])
        a_mat[un] = jnp.where(tri, _bdot(qe[rs, kl], ke[rs, kl], _NT), 0.0)
    for un in units:
        rs, vl = rows_of(un[0], un[1]), vl_of(un[2])
        o_in[un] = _bdot(a_mat[un], v[rs, vl])
    for un in units:
        rs, kl, vl = rows_of(un[0], un[1]), kl_of(un[2]), vl_of(un[2])
        ds[un] = _bdot(kd[rs, kl], v[rs, vl], _TN)
    for ci in range(NC):
        cur = [(nb, ci, h) for nb in range(NB) for h in range(GLA_HEADS)]
        o = {}
        for un in cur:
            rs, kl = rows_of(un[0], un[1]), kl_of(un[2])
            o[un] = o_in[un] + _bdot(qe[rs, kl], s_scr[un[0], un[2]])
        for un in cur:
            rs, kl = rows_of(un[0], un[1]), kl_of(un[2])
            dcol = jnp.transpose(jnp.broadcast_to(decay[rs.start:rs.start + 1, kl], (GLA_HK, GLA_HK)))
            s_scr[un[0], un[2]] = s_scr[un[0], un[2]] * jnp.concatenate([dcol, dcol], axis=1) + ds[un]
        for un in cur:
            rs, vl = rows_of(un[0], un[1]), vl_of(un[2])
            o_scr[rs, vl] = _layernorm_lanes(o[un], lnw_ref[:, vl], lnb_ref[:, vl], LN_EPS) * silu_r[rs, vl]
    _store_rows(o_ref, o_scr[...], NB, TB, decode)

    @pl.when(blk == n_blk - 1)
    def _fin():
        sout_ref[...] = s_scr[...]


def gla_branch(pm, pxg, s0, prm, grp, n_tok, o_buf=None, s_buf=None):
    row = lambda w: pl.BlockSpec((1, w), lambda b, c: (0, 0))
    extra = [a for a in (o_buf, s_buf) if a is not None]
    n_in = 7
    aliases = {}
    if o_buf is not None:
        aliases[n_in] = 0
    if s_buf is not None:
        aliases[n_in + len(extra) - 1] = 1
    kern = functools.partial(_gla_kernel, C=grp.C, NB=grp.NB, NC=grp.NC, t_valid=grp.T, decode=grp.decode,
                             n_alias=len(extra))
    return pl.pallas_call(
        kern,
        grid=grp.grid,
        in_specs=[
            grp.rows(3 * BRANCH_W),
            grp.rows(LANES),
            grp.state(GLA_HEADS, GLA_HK, GLA_HV),
            pl.BlockSpec((LANES, GLA_DK), lambda b, c: (0, 0)),
            row(GLA_DK), row(BRANCH_W), row(BRANCH_W),
        ] + _alias_args(extra),
        out_specs=[
            grp.rows(BRANCH_W),
            grp.state(GLA_HEADS, GLA_HK, GLA_HV),
        ],
        out_shape=[
            jax.ShapeDtypeStruct((n_tok, BRANCH_W), F32),
            grp.state_shape(GLA_HEADS, GLA_HK, GLA_HV),
        ],
        scratch_shapes=[pltpu.VMEM((grp.NB, GLA_HEADS, GLA_HK, GLA_HV), F32),
                        pltpu.VMEM((grp.NB * grp.TB, BRANCH_W), F32)],
        input_output_aliases=aliases,
        compiler_params=_params("parallel", "arbitrary"),
        name="gla",
    )(pm, pxg, s0, prm["gw2"], prm["gb"], prm["lnw"], prm["lnb"], *extra)


def _cmlp_kernel(*refs, C, NB, decode, n_alias):
    p_ref, lnw_ref, lnb_ref, ws_ref, bt_ref = refs[:5]
    o_ref, v_ref, o_scr = refs[5 + n_alias:]
    c = pl.program_id(1)
    n_chunks = pl.num_programs(1)
    z = _gelu(_load_rows(p_ref, NB, C, decode))
    u = z[:, 0:CM_W]
    vn = _layernorm_lanes(z[:, CM_W:2 * CM_W], lnw_ref[...], lnb_ref[...], LN_EPS)
    tri = _iota((C, C), 0) >= _iota((C, C), 1)
    for g in range(CM_GROUPS):
        gl = slice(CM_GW * g, CM_GW * (g + 1))
        ws_c = jnp.where(tri, ws_ref[g], 0.0)
        for nb in range(NB):
            rs = slice(nb * C, (nb + 1) * C)
            mixed = _bdot(ws_c, vn[rs, gl]) + bt_ref[:, g:g + 1]
            o_scr[rs, gl] = u[rs, gl] * mixed
    _store_rows(o_ref, o_scr[...], NB, C, decode)

    @pl.when(c == n_chunks - 1)
    def _fin():
        v_ref[...] = vn.reshape(NB, C, CM_W)


def cmlp_branch(p, prm, grp, n_tok, o_buf=None):
    row = lambda w: pl.BlockSpec((1, w), lambda b, c: (0, 0))
    extra = [a for a in (o_buf,) if a is not None]
    C = grp.TB
    return pl.pallas_call(
        functools.partial(_cmlp_kernel, C=C, NB=grp.NB, decode=grp.decode, n_alias=len(extra)),
        grid=grp.grid,
        in_specs=[
            grp.rows(2 * CM_W),
            row(CM_W), row(CM_W),
            pl.BlockSpec((CM_GROUPS, C, C), lambda b, c: (0, 0, 0)),
            pl.BlockSpec((C, CM_GROUPS), lambda b, c: (0, 0)),
        ] + _alias_args(extra),
        out_specs=[
            grp.rows(CM_W),
            pl.BlockSpec((grp.NB, C, CM_W), lambda b, c: (b, 0, 0)),
        ],
        out_shape=[
            jax.ShapeDtypeStruct((n_tok, CM_W), F32),
            jax.ShapeDtypeStruct((grp.B, C, CM_W), F32),
        ],
        scratch_shapes=[pltpu.VMEM((grp.NB * C, CM_W), F32)],
        input_output_aliases={5: 0} if extra else {},
        compiler_params=_params("parallel", "arbitrary"),
        name="cmlp",
    )(p, prm["lnw"], prm["lnb"], prm["ws"], prm["bt"], *extra)


def _mix_kernel(orw_ref, ogl_ref, ocm_ref, g0_ref, g1_ref, g2_ref, wb_ref, o_ref):
    acc = _sigmoid(g0_ref[...]) * _bdot(orw_ref[...], wb_ref[0])
    acc += _sigmoid(g1_ref[...]) * _bdot(ogl_ref[...], wb_ref[1])
    acc += _sigmoid(g2_ref[...]) * _bdot(ocm_ref[...], wb_ref[2])
    o_ref[...] = acc.astype(o_ref.dtype)


def branch_mix(o_rw, o_gla, o_cm, p_gate, wb, tm, tn):
    n = o_rw.shape[0]
    nj = D_MODEL // tn
    ospec = pl.BlockSpec((tm, BRANCH_W), lambda i, j: (i, 0))
    gspec = lambda g: pl.BlockSpec((tm, tn), lambda i, j: (i, g * nj + j))
    return pl.pallas_call(
        _mix_kernel,
        grid=(n // tm, nj),
        in_specs=[ospec, ospec, ospec, gspec(0), gspec(1), gspec(2),
                  pl.BlockSpec((3, BRANCH_W, tn), lambda i, j: (0, 0, j))],
        out_specs=pl.BlockSpec((tm, tn), lambda i, j: (i, j)),
        out_shape=jax.ShapeDtypeStruct((n, D_MODEL), BF16),
        compiler_params=_params("parallel", "parallel"),
        name="branch_mix",
    )(o_rw, o_gla, o_cm, p_gate, p_gate, p_gate, wb)


def _merge_network(lo, hi, r):
    step = r * 2
    if step < hi - lo:
        yield from _merge_network(lo, hi, step)
        yield from _merge_network(lo + r, hi, step)
        yield from [(i, i + r) for i in range(lo + r, hi - r, step)]
    else:
        yield (lo, lo + r)


def _sort_network(lo, hi):
    if hi - lo >= 1:
        mid = lo + (hi - lo) // 2
        yield from _sort_network(lo, mid)
        yield from _sort_network(mid + 1, hi)
        yield from _merge_network(lo, hi, 1)


def _pop_top(lists, singles, n):
    lists = list(lists)
    depth = len(lists)
    out = []
    for r in range(n):
        m = jnp.max(lists[0], axis=0, keepdims=True)
        if singles is not None:
            m = jnp.maximum(m, jnp.max(singles, axis=0, keepdims=True))
        out.append(m)
        need = n - 1 - r
        if need == 0:
            break
        hit = lists[0] == m
        for d in range(min(depth, need)):
            lists[d] = jnp.where(hit, lists[d + 1] if d + 1 < depth else NEG_INF, lists[d])
        if singles is not None:
            singles = jnp.where(singles == m, NEG_INF, singles)
    return out


def _top_rows(s, n):
    slabs = [s[SUBLANES * v:SUBLANES * (v + 1)] for v in range(s.shape[0] // SUBLANES)]
    for i, j in _sort_network(0, len(slabs) - 1):
        slabs[i], slabs[j] = jnp.maximum(slabs[i], slabs[j]), jnp.minimum(slabs[i], slabs[j])
    return _pop_top(slabs, None, n)


def _router_kernel(q_ref, keys_ref, beta_ref, thr_ref, c_ref):
    tn = q_ref.shape[0]
    n_top = PEER_TOPK + 1
    for h in range(PEER_HEADS):
        s1 = _hdot(keys_ref[2 * h], q_ref[:, LANES * (2 * h):LANES * (2 * h + 1)], _NT)
        s2 = _hdot(keys_ref[2 * h + 1], q_ref[:, LANES * (2 * h + 1):LANES * (2 * h + 2)], _NT)
        a = _top_rows(s1, n_top)
        b = _top_rows(s2, n_top)
        pad = jnp.full((3 * SUBLANES - n_top, tn), NEG_INF, F32)
        a_all = jnp.concatenate(a + [pad], axis=0)
        best = _pop_top([a_all[0:SUBLANES] + b[y] for y in range(n_top)], a_all[SUBLANES:] + b[0], n_top)
        top = best[0]
        zsum = jnp.zeros_like(top)
        for t in range(PEER_TOPK):
            zsum += jnp.exp(best[t] - top)
        tau = 0.5 * (best[PEER_TOPK - 1] + best[PEER_TOPK])
        rows = slice(SUBLANES * h, SUBLANES * (h + 1))
        beta_ref[h] = s2 - b[0]
        thr_ref[:, rows, :] = ((tau - top) - (s1 - a[0])).reshape(PEER_ITILES, SUBLANES, tn)
        c_ref[:, rows, :] = (jnp.exp(s1 - a[0]) / zsum).reshape(PEER_ITILES, SUBLANES, tn)


def peer_router(q, keys, tn):
    n = q.shape[0]
    spec = pl.BlockSpec((PEER_HEADS, PEER_NKEYS, tn), lambda i: (0, 0, i))
    shp = jax.ShapeDtypeStruct((PEER_HEADS, PEER_NKEYS, n), F32)
    spec_i = pl.BlockSpec((PEER_ITILES, PEER_HEADS * SUBLANES, tn), lambda i: (0, 0, i))
    shp_i = jax.ShapeDtypeStruct((PEER_ITILES, PEER_HEADS * SUBLANES, n), F32)
    return pl.pallas_call(
        _router_kernel,
        grid=(n // tn,),
        in_specs=[pl.BlockSpec((tn, D_MODEL), lambda i: (i, 0)),
                  pl.BlockSpec((2 * PEER_HEADS, PEER_NKEYS, PEER_DH), lambda i: (0, 0, 0))],
        out_specs=[spec, spec_i, spec_i],
        out_shape=[shp, shp_i, shp_i],
        compiler_params=_params("parallel"),
        name="peer_router",
    )(q, keys)


def _peer_kernel(xn_ref, u_ref, v_ref, beta_ref, thr_ref, c_ref, res_ref, o_ref, eb_scr, *, ti):
    e = pl.program_id(1)
    sub_i = PEER_SUB_KEYS

    @pl.when(e == 0)
    def _init():
        o_ref[...] = res_ref[...]
        eb_scr[...] = jnp.exp(beta_ref[...])

    sub = (e % (SUBLANES // ti)) * ti
    xn = xn_ref[...]
    starts = list(range(0, ti, sub_i))

    def scores(s0):
        rows = slice(s0 * PEER_NKEYS, (s0 + sub_i) * PEER_NKEYS)
        return lax.dot_general(u_ref[rows, :], xn, _NT, preferred_element_type=F32)

    def weights(s0, ht):
        gates = []
        for ii in range(s0, s0 + sub_i):
            acc = None
            for h in range(PEER_HEADS):
                r = pl.ds(SUBLANES * h + sub + ii, 1)
                hit = beta_ref[h] >= thr_ref[0, r, :]
                term = jnp.where(hit, eb_scr[h] * c_ref[0, r, :], 0.0)
                acc = term if acc is None else acc + term
            gates.append(acc)
        g = jnp.concatenate(gates, axis=0) if len(gates) > 1 else gates[0]
        return (g * _gelu(ht)).astype(BF16)

    tot = None
    ht_next = scores(starts[0])
    for n, s0 in enumerate(starts):
        ht = ht_next
        if n + 1 < len(starts):
            ht_next = scores(starts[n + 1])
        rows = slice(s0 * PEER_NKEYS, (s0 + sub_i) * PEER_NKEYS)
        part = lax.dot_general(weights(s0, ht), v_ref[rows, :], _TN, preferred_element_type=F32)
        tot = part if tot is None else tot + part
    o_ref[...] += tot


def peer_experts(xn, u, v, layer, beta, thr, cc, res, tm, ti):
    n = xn.shape[0]
    te = ti * PEER_NKEYS
    per = SUBLANES // ti
    once = dict(pipeline_mode=pl.Buffered(1))
    ispec = pl.BlockSpec((1, PEER_HEADS * SUBLANES, tm), lambda i, e: (e // per, 0, i))
    return pl.pallas_call(
        functools.partial(_peer_kernel, ti=ti),
        grid=(n // tm, PEER_NKEYS // ti),
        in_specs=[
            pl.BlockSpec((tm, D_MODEL), lambda i, e: (i, 0), **once),
            pl.BlockSpec((None, te, D_MODEL), lambda i, e: (layer, e, 0)),
            pl.BlockSpec((None, te, D_MODEL), lambda i, e: (layer, e, 0)),
            pl.BlockSpec((PEER_HEADS, PEER_NKEYS, tm), lambda i, e: (0, 0, i), **once),
            ispec, ispec,
            pl.BlockSpec((tm, D_MODEL), lambda i, e: (i, 0), **once),
        ],
        out_specs=pl.BlockSpec((tm, D_MODEL), lambda i, e: (i, 0)),
        out_shape=jax.ShapeDtypeStruct((n, D_MODEL), F32),
        scratch_shapes=[pltpu.VMEM((PEER_HEADS, PEER_NKEYS, tm), F32)],
        compiler_params=_params("parallel", "arbitrary"),
        name="peer_experts",
    )(xn, u, v, beta, thr, cc, res)


TOKEN_TILE = 640
PROMPT_CHUNKS_PER_STEP = 4
DECODE_ROWS_PER_STEP = 8
PEER_KEYS_PER_STEP = 8
W_IN_COLS = dict(rw=(0, 3328), gla_qkv=(3328, 5376), gla_xg=(5376, 5392), gla_r=(5392, 6416),
                 cm=(6416, 8464), gate=(8464, 14608))


def _proj(hn, w, tn):
    return matmul(hn, w.astype(BF16), TOKEN_TILE, tn)


def kernel(x_prompt, x_sample, state_rw_shift, state_rwkv, state_gla, norm1, w_in, rw_mu, rw_w0, rw_w2, rw_a0, rw_a2, rw_g2, rw_kk, rw_ka, rw_rk, rw_lnw, rw_lnb, gla_gw2, gla_gb, gla_lnw, gla_lnb, cm_lnw, cm_lnb, cm_ws, cm_b, w_branch, w_out, norm2, peer_wq, peer_keys, peer_u, peer_v, norm_f):
    bp, tp, d = x_prompt.shape
    bs = x_sample.shape[0]
    n_p = bp * tp
    n_tok = n_p + bs
    x = jnp.concatenate([x_prompt.reshape(n_p, d), x_sample.reshape(bs, d)], axis=0)
    head_of_lane = jnp.arange(LANES) // RW_HEAD
    ones_bd = (head_of_lane[:, None] == head_of_lane[None, :]).astype(F32)
    row = lambda a: a.reshape(1, -1)
    dec_steps = SUBLANES
    cm_chunk_p = min(CM_CHUNK, tp)
    peer_u_bf = peer_u.astype(BF16)
    peer_v_bf = peer_v.astype(BF16)
    zero_shift = jnp.zeros((1, bp, 1, RW_COLS), F32)
    zero_rw = jnp.zeros((1, bp, RW_HEADS, RW_HEAD, RW_HEAD), F32)
    zero_gla = jnp.zeros((1, bp, GLA_HEADS, GLA_HK, GLA_HV), F32)
    shift_in = state_rw_shift[:, :, None, :]
    outs = {k: [] for k in ("p_sh", "p_rw", "p_gla", "p_cm", "s_sh", "s_cm")}
    s_rw = s_gla = None
    for l in range(DEPTH):
        prompt = lambda chunk, nc: _Group(bp, tp, chunk, 1, nc, False, 0, 0, 1)
        sample = _Group(bs, 1, dec_steps, DECODE_ROWS_PER_STEP, 1, True, n_p, l, DEPTH)
        wl = w_in[l]
        cols = lambda name: wl[:, W_IN_COLS[name][0]:W_IN_COLS[name][1]]
        hn = rmsnorm(x, norm1[l], BF16, TOKEN_TILE)
        p_rw = _proj(hn, cols("rw"), 1664)
        p_gl = _proj(hn, jnp.concatenate([cols("gla_qkv"), cols("gla_r")], axis=1), 1536)
        p_xg = _proj(hn, jnp.pad(cols("gla_xg"), ((0, 0), (0, LANES - GLA_GATE_LORA))), LANES)
        p_cm = _proj(hn, cols("cm"), 1024)
        p_gt = _proj(hn, cols("gate"), 1536)

        wa2 = jnp.zeros((LANES, 2 * BRANCH_W), F32)
        wa2 = wa2.at[0:RW_HEAD, 0:BRANCH_W].set(rw_w2[l]).at[RW_HEAD:LANES, BRANCH_W:].set(rw_a2[l])
        rw_prm = dict(mu=row(rw_mu[l]), w0=row(rw_w0[l]), wa2=wa2.astype(BF16), a0=row(rw_a0[l]),
                      g2=rw_g2[l].astype(BF16), kk=row(rw_kk[l]), ka=row(rw_ka[l]), rk=row(rw_rk[l]),
                      lnw=row(rw_lnw[l]), lnb=row(rw_lnb[l]), ones_bd=ones_bd)
        o_rw, sh_p, st_p = rwkv_branch(p_rw, zero_shift, zero_rw, rw_prm,
                                       prompt(RW_CHUNK, PROMPT_CHUNKS_PER_STEP), n_tok)
        o_rw, sh_s, s_rw = rwkv_branch(p_rw, shift_in, state_rwkv, rw_prm, sample, n_tok, o_buf=o_rw, s_buf=s_rw)
        outs["p_sh"].append(sh_p[:, 0])
        outs["p_rw"].append(st_p[0])
        outs["s_sh"].append(sh_s[:, 0])

        gla_prm = dict(gw2=jnp.pad(gla_gw2[l], ((0, LANES - GLA_GATE_LORA), (0, 0))).astype(BF16),
                       gb=row(gla_gb[l]), lnw=row(gla_lnw[l]), lnb=row(gla_lnb[l]))
        o_gla, st_p = gla_branch(p_gl, p_xg, zero_gla, gla_prm, prompt(GLA_CHUNK, PROMPT_CHUNKS_PER_STEP), n_tok)
        o_gla, s_gla = gla_branch(p_gl, p_xg, state_gla, gla_prm, sample, n_tok, o_buf=o_gla, s_buf=s_gla)
        outs["p_gla"].append(st_p[0])

        cm_prm = lambda c: dict(lnw=row(cm_lnw[l]), lnb=row(cm_lnb[l]), ws=cm_ws[l][:, :c, :c],
                                bt=jnp.transpose(cm_b[l])[:c])
        o_cm, v_p = cmlp_branch(p_cm, cm_prm(cm_chunk_p), prompt(cm_chunk_p, 1), n_tok)
        o_cm, v_s = cmlp_branch(p_cm, cm_prm(dec_steps), sample, n_tok, o_buf=o_cm)
        outs["p_cm"].append(v_p[:, :tp - ((tp - 1) // CM_CHUNK) * CM_CHUNK])
        outs["s_cm"].append(v_s[:, 0:1])

        mix = branch_mix(o_rw, o_gla, o_cm, p_gt, w_branch[l].astype(BF16), TOKEN_TILE, 1024)
        x = matmul(mix, w_out[l].astype(BF16), TOKEN_TILE, 1024, res=x)

        hf = rmsnorm(x, norm2[l], BF16, TOKEN_TILE)
        q = matmul(hf, peer_wq[l].astype(BF16), TOKEN_TILE, 1024)
        beta, thr, cc = peer_router(q, peer_keys[l].reshape(2 * PEER_HEADS, PEER_NKEYS, PEER_DH), TOKEN_TILE)
        x = peer_experts(hf, peer_u_bf, peer_v_bf, l, beta, thr, cc, x, TOKEN_TILE, PEER_KEYS_PER_STEP)

    y = rmsnorm(x, norm_f, F32, TOKEN_TILE)
    st = lambda k: jnp.stack(outs[k])
    return (y[:n_p].reshape(bp, tp, d), y[n_p:].reshape(bs, 1, d), st("p_sh"), st("p_rw"), st("p_gla"), st("p_cm"),
            st("s_sh"), s_rw, s_gla, st("s_cm"))
```

```python
import functools
import math

import jax
import jax.numpy as jnp
from jax import lax
from jax.experimental import pallas as pl
from jax.experimental.pallas import tpu as pltpu

F32 = jnp.float32
BF16 = jnp.bfloat16

D_MODEL = 2048
DEPTH = 4
BRANCH_W = 1024
RW_HEAD = 64
RW_HEADS = 16
RW_PAIRS = RW_HEADS // 2
RW_COLS = 3328
RW_GN_EPS = 64e-5
RW_CHUNK = 64
GLA_HEADS = 4
GLA_HK = 128
GLA_HV = 256
GLA_DK = 512
GLA_GATE_LORA = 16
GLA_TAU = 16.0
GLA_CHUNK = 64
CM_W = 1024
CM_CHUNK = 128
CM_GROUPS = 8
CM_GW = 128
PEER_HEADS = 8
PEER_NKEYS = 128
PEER_N = PEER_NKEYS * PEER_NKEYS
PEER_DH = 128
PEER_TOPK = 16
PEER_ITILES = 16
PEER_SUB_KEYS = 2
RMS_EPS = 1e-6
LN_EPS = 1e-5

LANES = 128
SUBLANES = 8
VMEM_LIMIT_BYTES = 56 * 1024 * 1024

NEG_INF = float("-inf")

_NN = (((1,), (0,)), ((), ()))
_NT = (((1,), (1,)), ((), ()))
_TN = (((0,), (0,)), ((), ()))


def _params(*sem):
    return pltpu.CompilerParams(dimension_semantics=sem, vmem_limit_bytes=VMEM_LIMIT_BYTES)


def _bdot(a, b, dims=_NN):
    return lax.dot_general(a.astype(BF16), b.astype(BF16), dims, preferred_element_type=F32)


def _hdot(a, b, dims=_NN):
    return lax.dot_general(a.astype(F32), b.astype(F32), dims, preferred_element_type=F32,
                           precision=lax.Precision.HIGHEST)


def _gelu(x):
    return 0.5 * x * (1.0 + lax.erf(x * (1.0 / math.sqrt(2.0))))


def _sigmoid(x):
    return 1.0 / (1.0 + jnp.exp(-x))


def _softplus(x):
    return jnp.maximum(x, 0.0) + jnp.log(1.0 + jnp.exp(-jnp.abs(x)))


def _iota(shape, dim):
    return lax.broadcasted_iota(jnp.int32, shape, dim)


def _rmsnorm_kernel(x_ref, g_ref, o_ref):
    x = x_ref[...]
    ms = jnp.mean(x * x, axis=-1, keepdims=True)
    o_ref[...] = (x * lax.rsqrt(ms + RMS_EPS) * g_ref[...]).astype(o_ref.dtype)


def rmsnorm(x, g, out_dtype, tm):
    n, d = x.shape
    assert n % tm == 0
    return pl.pallas_call(
        _rmsnorm_kernel,
        grid=(n // tm,),
        in_specs=[pl.BlockSpec((tm, d), lambda i: (i, 0)), pl.BlockSpec((1, d), lambda i: (0, 0))],
        out_specs=pl.BlockSpec((tm, d), lambda i: (i, 0)),
        out_shape=jax.ShapeDtypeStruct((n, d), out_dtype),
        compiler_params=_params("parallel"),
        name="rmsnorm",
    )(x, g.reshape(1, d))


def _matmul_kernel(a_ref, w_ref, o_ref):
    o_ref[...] = jnp.dot(a_ref[...], w_ref[...], preferred_element_type=F32)


def _matmul_res_kernel(a_ref, w_ref, r_ref, o_ref):
    o_ref[...] = r_ref[...] + jnp.dot(a_ref[...], w_ref[...], preferred_element_type=F32)


def matmul(a, w, tm, tn, res=None):
    n, k = a.shape
    m = w.shape[1]
    assert n % tm == 0 and m % tn == 0
    in_specs = [pl.BlockSpec((tm, k), lambda j, i: (i, 0)), pl.BlockSpec((k, tn), lambda j, i: (0, j))]
    args = [a, w]
    body = _matmul_kernel
    if res is not None:
        in_specs.append(pl.BlockSpec((tm, tn), lambda j, i: (i, j)))
        args.append(res)
        body = _matmul_res_kernel
    return pl.pallas_call(
        body,
        grid=(m // tn, n // tm),
        in_specs=in_specs,
        out_specs=pl.BlockSpec((tm, tn), lambda j, i: (i, j)),
        out_shape=jax.ShapeDtypeStruct((n, m), F32),
        compiler_params=_params("parallel", "parallel"),
        name="matmul",
    )(*args)


def _split3(x):
    hi = x.astype(BF16)
    r1 = x - hi.astype(F32)
    mid = r1.astype(BF16)
    lo = (r1 - mid.astype(F32)).astype(BF16)
    return hi, mid, lo


def _exact_lhs_dot(a01, x, dims=_NN):
    a = a01.astype(BF16)
    hi, mid, lo = _split3(x)
    d = lambda y: lax.dot_general(a, y, dims, preferred_element_type=F32)
    return d(hi) + (d(mid) + d(lo))


def _exact_rhs_dot(x, b01, dims=_NN):
    b = b01.astype(BF16)
    hi, mid, lo = _split3(x)
    d = lambda y: lax.dot_general(y, b, dims, preferred_element_type=F32)
    return d(hi) + (d(mid) + d(lo))


def _chunk_cumsum(x, C):
    R, W = x.shape
    SL = max(C, 64)
    rr = _iota((SL, SL), 0)
    cc = _iota((SL, SL), 1)
    tri = ((rr // C == cc // C) & (rr >= cc)).astype(F32)
    same_chunk = (rr // C == cc // C).astype(F32)
    is_last = (_iota((SL, 1), 0) % C) == (C - 1)
    cums, ends = [], []
    for s0 in range(0, R, SL):
        cum_s = _exact_lhs_dot(tri, x[s0:s0 + SL])
        cums.append(cum_s)
        if C == SL:
            ends.append(jnp.broadcast_to(cum_s[C - 1:C], (SL, W)))
        else:
            ends.append(_exact_lhs_dot(same_chunk, jnp.where(is_last, cum_s, 0.0)))
    cum = jnp.concatenate(cums, axis=0) if len(cums) > 1 else cums[0]
    cum_end = jnp.concatenate(ends, axis=0) if len(ends) > 1 else ends[0]
    return cum, cum_end


def _pair_stack(x, lane_a):
    return jnp.concatenate([jnp.where(lane_a, x, 0.0), jnp.where(lane_a, 0.0, x)], axis=0)


def _pair_stack_cat(x, C):
    first = _iota((1, 2 * C), 1) < C
    return jnp.concatenate([jnp.where(first, x, 0.0), jnp.where(first, 0.0, x)], axis=0)


def _load_rows(ref, NB, TB, decode):
    x = ref[...]
    if not decode:
        return x
    w = x.shape[-1]
    t_loc = _iota((NB * TB, 1), 0) % TB
    return jnp.where(t_loc == 0, jnp.broadcast_to(x[:, None, :], (NB, TB, w)).reshape(NB * TB, w), 0.0)


def _store_rows(ref, val, NB, TB, decode):
    ref[...] = val.reshape(NB, TB, val.shape[-1])[:, 0, :] if decode else val


def _row_spec(width, NB, TB, n_blk, decode, row0):
    if decode:
        return pl.BlockSpec((NB, width), lambda b, c: (row0 // NB + b, 0))
    return pl.BlockSpec((TB, width), lambda b, c: (b * n_blk + c, 0))


def _rwkv_kernel(*refs, C, NB, NC, t_valid, decode, n_alias):
    (p_ref, prev_ref, s0_ref, mu_ref, w0_ref, wa2_ref, a0_ref, g2_ref, kk_ref, ka_ref, rk_ref,
     lnw_ref, lnb_ref, ones_ref) = refs[:14]
    o_ref, shift_ref, sout_ref, s_scr, prev_scr, y_scr = refs[14 + n_alias:]
    blk = pl.program_id(1)
    n_blk = pl.num_programs(1)
    H = RW_HEAD
    TB = NC * C
    R = NB * TB

    @pl.when(blk == 0)
    def _init():
        prev_scr[...] = prev_ref[:, 0, :]
        s_scr[...] = jnp.zeros_like(s_scr)
        for nb in range(NB):
            for j in range(RW_PAIRS):
                s_scr[nb, j, 0:H, 0:H] = s0_ref[nb, 2 * j]
                s_scr[nb, j, H:2 * H, H:2 * H] = s0_ref[nb, 2 * j + 1]

    p = _load_rows(p_ref, NB, TB, decode)
    t_loc = _iota((R, 1), 0) % TB
    carried = jnp.broadcast_to(prev_scr[...][:, None, :], (NB, TB, RW_COLS)).reshape(R, RW_COLS)
    prev = jnp.where(t_loc == 0, carried, pltpu.roll(p, 1, axis=0))
    last_row = (t_valid - 1) % TB
    p_last = p.reshape(NB, TB, RW_COLS)[:, last_row, :]
    prev_scr[...] = p_last
    xs = p + (prev - p) * mu_ref[...]
    r = xs[:, 0:1024]
    k = xs[:, 1024:2048]
    v = xs[:, 2048:3072]
    z = xs[:, 3072:3200]
    xg = xs[:, 3200:3328]
    lane128 = _iota((1, LANES), 1)
    tz = jnp.where(lane128 < RW_HEAD, jnp.tanh(z), z)
    wa = _bdot(tz, wa2_ref[...])
    w_lin = w0_ref[...] + wa[:, 0:1024]
    log_w = -jnp.exp(-_softplus(-w_lin) - 0.5)
    a = _sigmoid(a0_ref[...] + wa[:, 1024:2048])
    g = _bdot(_sigmoid(xg), g2_ref[...])
    ones_bd = ones_ref[...]

    def headsum(x):
        return jnp.concatenate(
            [_exact_rhs_dot(x[:, LANES * j:LANES * (j + 1)], ones_bd) for j in range(RW_PAIRS)], axis=1)

    kk = k * kk_ref[...]
    k_hat = k * (1.0 + (a - 1.0) * ka_ref[...])
    sums = headsum(jnp.concatenate([kk * kk, r * k_hat * rk_ref[...]], axis=0))
    kk = kk * lax.rsqrt(jnp.maximum(sums[0:R], 1e-24))
    bonus = sums[R:2 * R] * v
    b = kk * a
    if t_valid % TB != 0:
        valid = (blk * TB + t_loc) < t_valid
        log_w = jnp.where(valid, log_w, 0.0)
        kk = jnp.where(valid, kk, 0.0)
        b = jnp.where(valid, b, 0.0)
        k_hat = jnp.where(valid, k_hat, 0.0)
    cum, cum_end = _chunk_cumsum(log_w, C)
    kap_t = kk * jnp.exp(cum - log_w)
    r_t = r * jnp.exp(cum)
    inv = jnp.exp(-cum)
    b_t = b * inv
    k_t = k_hat * inv
    to_end = jnp.exp(cum_end - cum)
    b_e = b * to_end
    k_e = k_hat * to_end
    p_end = jnp.exp(cum_end)

    lane_a = lane128 < RW_HEAD
    tt = _iota((C, 2 * C), 0)
    ss = _iota((C, 2 * C), 1) % C
    strict = tt > ss
    lower = tt >= ss
    eye_cat = (tt == ss).astype(F32)
    blockdiag = (_iota((LANES, LANES), 0) < RW_HEAD) == (_iota((LANES, LANES), 1) < RW_HEAD)

    units = [(nb, ci, j) for nb in range(NB) for ci in range(NC) for j in range(RW_PAIRS)]

    def rows_of(nb, ci):
        r0 = (nb * NC + ci) * C
        return slice(r0, r0 + C)

    def lanes_of(j):
        return slice(LANES * j, LANES * (j + 1))

    kr, a1, a2, a3, a4, tm = {}, {}, {}, {}, {}, {}
    for un in units:
        rs, sl = rows_of(un[0], un[1]), lanes_of(un[2])
        kr[un] = jnp.concatenate([kap_t[rs, sl], r_t[rs, sl]], axis=0)
    for un in units:
        rs, sl = rows_of(un[0], un[1]), lanes_of(un[2])
        g1 = _bdot(kr[un], _pair_stack(b_t[rs, sl], lane_a), _NT)
        a1[un] = jnp.where(strict, g1[0:C], 0.0)
        a3[un] = jnp.where(lower, g1[C:2 * C], 0.0)
    for un in units:
        rs, sl = rows_of(un[0], un[1]), lanes_of(un[2])
        g2 = _bdot(kr[un], _pair_stack(k_t[rs, sl], lane_a), _NT)
        a2[un] = jnp.where(strict, g2[0:C], 0.0)
        a4[un] = jnp.where(lower, g2[C:2 * C], 0.0)
    for un in units:
        tm[un] = eye_cat - jnp.where((tt // 2 == ss // 2), a1[un], 0.0)
    m = 2
    while m < C:
        sel = (tt // (2 * m) == ss // (2 * m)) & ((tt // m) % 2 == 1) & ((ss // m) % 2 == 0)
        inner = {}
        for un in units:
            inner[un] = _bdot(jnp.where(sel, a1[un], 0.0), _pair_stack_cat(tm[un], C))
        for un in units:
            tm[un] = tm[un] - _bdot(tm[un], _pair_stack_cat(inner[un], C))
        m *= 2
    av = {}
    for un in units:
        rs, sl = rows_of(un[0], un[1]), lanes_of(un[2])
        av[un] = _bdot(jnp.concatenate([a2[un], a4[un]], axis=0), _pair_stack(v[rs, sl], lane_a))
    for ci in range(NC):
        cur = [(nb, ci, j) for nb in range(NB) for j in range(RW_PAIRS)]
        xs_s, u = {}, {}
        for un in cur:
            xs_s[un] = _bdot(kr[un], s_scr[un[0], un[2]], _NT)
        for un in cur:
            u[un] = -_bdot(tm[un], _pair_stack(xs_s[un][0:C] + av[un][0:C], lane_a))
        for un in cur:
            rs, sl = rows_of(un[0], un[1]), lanes_of(un[2])
            ds = _bdot(jnp.concatenate([u[un], v[rs, sl]], axis=0),
                       jnp.concatenate([b_e[rs, sl], k_e[rs, sl]], axis=0), _TN)
            s_scr[un[0], un[2]] = (s_scr[un[0], un[2]] * p_end[rs.start:rs.start + 1, sl]
                                   + jnp.where(blockdiag, ds, 0.0))
        for un in cur:
            rs, sl = rows_of(un[0], un[1]), lanes_of(un[2])
            y_scr[rs, sl] = (xs_s[un][C:2 * C] + av[un][C:2 * C]
                             + _bdot(a3[un], _pair_stack(u[un], lane_a)))

    y = y_scr[...]
    mean = headsum(y) * (1.0 / RW_HEAD)
    yc = y - mean
    var = headsum(yc * yc) * (1.0 / RW_HEAD)
    yn = yc * lax.rsqrt(var + RW_GN_EPS) * lnw_ref[...] + lnb_ref[...]
    _store_rows(o_ref, (yn + bonus) * g, NB, TB, decode)

    @pl.when(blk == n_blk - 1)
    def _fin():
        shift_ref[:, 0, :] = p_last
        for nb in range(NB):
            for j in range(RW_PAIRS):
                sout_ref[nb, 2 * j] = s_scr[nb, j, 0:H, 0:H]
                sout_ref[nb, 2 * j + 1] = s_scr[nb, j, H:2 * H, H:2 * H]


class _Group:
    def __init__(self, batch, steps, chunk, nb, nc, decode, row0, layer, n_layers):
        self.B, self.T, self.C, self.NB, self.NC = batch, steps, chunk, nb, nc
        self.decode, self.row0, self.layer, self.n_layers = decode, row0, layer, n_layers
        self.TB = nc * chunk
        assert batch % nb == 0 and (decode or steps % self.TB == 0)
        self.n_blk = 1 if decode else steps // self.TB
        self.grid = (batch // nb, self.n_blk)

    def rows(self, width):
        return _row_spec(width, self.NB, self.TB, self.n_blk, self.decode, self.row0)

    def state(self, *dims):
        zeros = (0,) * len(dims)
        return pl.BlockSpec((None, self.NB) + dims, lambda b, c, l=self.layer: (l, b) + zeros)

    def state_shape(self, *dims):
        return jax.ShapeDtypeStruct((self.n_layers, self.B) + dims, F32)


def _alias_args(extra):
    return [pl.BlockSpec(memory_space=pl.ANY) for _ in extra]


def rwkv_branch(p, shift_prev, s0, prm, grp, n_tok, o_buf=None, s_buf=None):
    row = lambda w: pl.BlockSpec((1, w), lambda b, c: (0, 0))
    full = lambda a: pl.BlockSpec(a.shape, lambda b, c: (0,) * a.ndim)
    extra = [a for a in (o_buf, s_buf) if a is not None]
    n_in = 14
    aliases = {}
    if o_buf is not None:
        aliases[n_in] = 0
    if s_buf is not None:
        aliases[n_in + len(extra) - 1] = 2
    kern = functools.partial(_rwkv_kernel, C=grp.C, NB=grp.NB, NC=grp.NC, t_valid=grp.T, decode=grp.decode,
                             n_alias=len(extra))
    return pl.pallas_call(
        kern,
        grid=grp.grid,
        in_specs=[
            grp.rows(RW_COLS),
            grp.state(1, RW_COLS),
            grp.state(RW_HEADS, RW_HEAD, RW_HEAD),
            row(RW_COLS), row(1024), full(prm["wa2"]), row(1024), full(prm["g2"]),
            row(1024), row(1024), row(1024), row(1024), row(1024), full(prm["ones_bd"]),
        ] + _alias_args(extra),
        out_specs=[
            grp.rows(BRANCH_W),
            pl.BlockSpec((grp.NB, 1, RW_COLS), lambda b, c: (b, 0, 0)),
            grp.state(RW_HEADS, RW_HEAD, RW_HEAD),
        ],
        out_shape=[
            jax.ShapeDtypeStruct((n_tok, BRANCH_W), F32),
            jax.ShapeDtypeStruct((grp.B, 1, RW_COLS), F32),
            grp.state_shape(RW_HEADS, RW_HEAD, RW_HEAD),
        ],
        scratch_shapes=[
            pltpu.VMEM((grp.NB, RW_PAIRS, LANES, LANES), F32),
            pltpu.VMEM((grp.NB, RW_COLS), F32),
            pltpu.VMEM((grp.NB * grp.TB, BRANCH_W), F32),
        ],
        input_output_aliases=aliases,
        compiler_params=_params("parallel", "arbitrary"),
        name="rwkv7",
    )(p, shift_prev, s0, prm["mu"], prm["w0"], prm["wa2"], prm["a0"], prm["g2"], prm["kk"], prm["ka"],
      prm["rk"], prm["lnw"], prm["lnb"], prm["ones_bd"], *extra)


def _layernorm_lanes(x, w, b, eps):
    mean = jnp.mean(x, axis=-1, keepdims=True)
    xc = x - mean
    var = jnp.mean(xc * xc, axis=-1, keepdims=True)
    return xc * lax.rsqrt(var + eps) * w + b


def _gla_kernel(*refs, C, NB, NC, t_valid, decode, n_alias):
    pm_ref, pxg_ref, s0_ref, gw2_ref, gb_ref, lnw_ref, lnb_ref = refs[:7]
    o_ref, sout_ref, s_scr, o_scr = refs[7 + n_alias:]
    blk = pl.program_id(1)
    n_blk = pl.num_programs(1)
    TB = NC * C
    R = NB * TB

    @pl.when(blk == 0)
    def _init():
        s_scr[...] = s0_ref[...]

    pm = _load_rows(pm_ref, NB, TB, decode)
    q = pm[:, 0:GLA_DK] * (GLA_HK ** -0.5)
    k = pm[:, GLA_DK:2 * GLA_DK]
    v = pm[:, 1024:2048]
    r = pm[:, 2048:3072]
    zl = _bdot(_load_rows(pxg_ref, NB, TB, decode), gw2_ref[...]) + gb_ref[...]
    log_a = (jnp.minimum(zl, 0.0) - jnp.log(1.0 + jnp.exp(-jnp.abs(zl)))) * (1.0 / GLA_TAU)
    if t_valid % TB != 0:
        t_loc = _iota((R, 1), 0) % TB
        log_a = jnp.where((blk * TB + t_loc) < t_valid, log_a, 0.0)
    cum, cum_end = _chunk_cumsum(log_a, C)
    qe = q * jnp.exp(cum)
    ke = k * jnp.exp(-cum)
    kd = k * jnp.exp(cum_end - cum)
    decay = jnp.exp(cum_end)
    silu_r = r * _sigmoid(r)
    tri = _iota((C, C), 0) >= _iota((C, C), 1)

    units = [(nb, ci, h) for nb in range(NB) for ci in range(NC) for h in range(GLA_HEADS)]
    rows_of = lambda nb, ci: slice((nb * NC + ci) * C, (nb * NC + ci + 1) * C)
    kl_of = lambda h: slice(GLA_HK * h, GLA_HK * (h + 1))
    vl_of = lambda h: slice(GLA_HV * h, GLA_HV * (h + 1))
    a_mat, o_in, ds = {}, {}, {}
    for un in units:
        rs, kl = rows_of(un[0], un[1]), kl_of(un[2])
        a_mat[un] = jnp.where(tri, _bdot(qe[rs, kl], ke[rs, kl], _NT), 0.0)
    for un in units:
        rs, vl = rows_of(un[0], un[1]), vl_of(un[2])
        o_in[un] = _bdot(a_mat[un], v[rs, vl])
    for un in units:
        rs, kl, vl = rows_of(un[0], un[1]), kl_of(un[2]), vl_of(un[2])
        ds[un] = _bdot(kd[rs, kl], v[rs, vl], _TN)
    for ci in range(NC):
        cur = [(nb, ci, h) for nb in range(NB) for h in range(GLA_HEADS)]
        o = {}
        for un in cur:
            rs, kl = rows_of(un[0], un[1]), kl_of(un[2])
            o[un] = o_in[un] + _bdot(qe[rs, kl], s_scr[un[0], un[2]])
        for un in cur:
            rs, kl = rows_of(un[0], un[1]), kl_of(un[2])
            dcol = jnp.transpose(jnp.broadcast_to(decay[rs.start:rs.start + 1, kl], (GLA_HK, GLA_HK)))
            s_scr[un[0], un[2]] = s_scr[un[0], un[2]] * jnp.concatenate([dcol, dcol], axis=1) + ds[un]
        for un in cur:
            rs, vl = rows_of(un[0], un[1]), vl_of(un[2])
            o_scr[rs, vl] = _layernorm_lanes(o[un], lnw_ref[:, vl], lnb_ref[:, vl], LN_EPS) * silu_r[rs, vl]
    _store_rows(o_ref, o_scr[...], NB, TB, decode)

    @pl.when(blk == n_blk - 1)
    def _fin():
        sout_ref[...] = s_scr[...]


def gla_branch(pm, pxg, s0, prm, grp, n_tok, o_buf=None, s_buf=None):
    row = lambda w: pl.BlockSpec((1, w), lambda b, c: (0, 0))
    extra = [a for a in (o_buf, s_buf) if a is not None]
    n_in = 7
    aliases = {}
    if o_buf is not None:
        aliases[n_in] = 0
    if s_buf is not None:
        aliases[n_in + len(extra) - 1] = 1
    kern = functools.partial(_gla_kernel, C=grp.C, NB=grp.NB, NC=grp.NC, t_valid=grp.T, decode=grp.decode,
                             n_alias=len(extra))
    return pl.pallas_call(
        kern,
        grid=grp.grid,
        in_specs=[
            grp.rows(3 * BRANCH_W),
            grp.rows(LANES),
            grp.state(GLA_HEADS, GLA_HK, GLA_HV),
            pl.BlockSpec((LANES, GLA_DK), lambda b, c: (0, 0)),
            row(GLA_DK), row(BRANCH_W), row(BRANCH_W),
        ] + _alias_args(extra),
        out_specs=[
            grp.rows(BRANCH_W),
            grp.state(GLA_HEADS, GLA_HK, GLA_HV),
        ],
        out_shape=[
            jax.ShapeDtypeStruct((n_tok, BRANCH_W), F32),
            grp.state_shape(GLA_HEADS, GLA_HK, GLA_HV),
        ],
        scratch_shapes=[pltpu.VMEM((grp.NB, GLA_HEADS, GLA_HK, GLA_HV), F32),
                        pltpu.VMEM((grp.NB * grp.TB, BRANCH_W), F32)],
        input_output_aliases=aliases,
        compiler_params=_params("parallel", "arbitrary"),
        name="gla",
    )(pm, pxg, s0, prm["gw2"], prm["gb"], prm["lnw"], prm["lnb"], *extra)


def _cmlp_kernel(*refs, C, NB, decode, n_alias):
    p_ref, lnw_ref, lnb_ref, ws_ref, bt_ref = refs[:5]
    o_ref, v_ref, o_scr = refs[5 + n_alias:]
    c = pl.program_id(1)
    n_chunks = pl.num_programs(1)
    z = _gelu(_load_rows(p_ref, NB, C, decode))
    u = z[:, 0:CM_W]
    vn = _layernorm_lanes(z[:, CM_W:2 * CM_W], lnw_ref[...], lnb_ref[...], LN_EPS)
    tri = _iota((C, C), 0) >= _iota((C, C), 1)
    for g in range(CM_GROUPS):
        gl = slice(CM_GW * g, CM_GW * (g + 1))
        ws_c = jnp.where(tri, ws_ref[g], 0.0)
        for nb in range(NB):
            rs = slice(nb * C, (nb + 1) * C)
            mixed = _bdot(ws_c, vn[rs, gl]) + bt_ref[:, g:g + 1]
            o_scr[rs, gl] = u[rs, gl] * mixed
    _store_rows(o_ref, o_scr[...], NB, C, decode)

    @pl.when(c == n_chunks - 1)
    def _fin():
        v_ref[...] = vn.reshape(NB, C, CM_W)


def cmlp_branch(p, prm, grp, n_tok, o_buf=None):
    row = lambda w: pl.BlockSpec((1, w), lambda b, c: (0, 0))
    extra = [a for a in (o_buf,) if a is not None]
    C = grp.TB
    return pl.pallas_call(
        functools.partial(_cmlp_kernel, C=C, NB=grp.NB, decode=grp.decode, n_alias=len(extra)),
        grid=grp.grid,
        in_specs=[
            grp.rows(2 * CM_W),
            row(CM_W), row(CM_W),
            pl.BlockSpec((CM_GROUPS, C, C), lambda b, c: (0, 0, 0)),
            pl.BlockSpec((C, CM_GROUPS), lambda b, c: (0, 0)),
        ] + _alias_args(extra),
        out_specs=[
            grp.rows(CM_W),
            pl.BlockSpec((grp.NB, C, CM_W), lambda b, c: (b, 0, 0)),
        ],
        out_shape=[
            jax.ShapeDtypeStruct((n_tok, CM_W), F32),
            jax.ShapeDtypeStruct((grp.B, C, CM_W), F32),
        ],
        scratch_shapes=[pltpu.VMEM((grp.NB * C, CM_W), F32)],
        input_output_aliases={5: 0} if extra else {},
        compiler_params=_params("parallel", "arbitrary"),
        name="cmlp",
    )(p, prm["lnw"], prm["lnb"], prm["ws"], prm["bt"], *extra)


def _mix_kernel(orw_ref, ogl_ref, ocm_ref, g0_ref, g1_ref, g2_ref, wb_ref, o_ref):
    acc = _sigmoid(g0_ref[...]) * _bdot(orw_ref[...], wb_ref[0])
    acc += _sigmoid(g1_ref[...]) * _bdot(ogl_ref[...], wb_ref[1])
    acc += _sigmoid(g2_ref[...]) * _bdot(ocm_ref[...], wb_ref[2])
    o_ref[...] = acc.astype(o_ref.dtype)


def branch_mix(o_rw, o_gla, o_cm, p_gate, wb, tm, tn):
    n = o_rw.shape[0]
    nj = D_MODEL // tn
    ospec = pl.BlockSpec((tm, BRANCH_W), lambda i, j: (i, 0))
    gspec = lambda g: pl.BlockSpec((tm, tn), lambda i, j: (i, g * nj + j))
    return pl.pallas_call(
        _mix_kernel,
        grid=(n // tm, nj),
        in_specs=[ospec, ospec, ospec, gspec(0), gspec(1), gspec(2),
                  pl.BlockSpec((3, BRANCH_W, tn), lambda i, j: (0, 0, j))],
        out_specs=pl.BlockSpec((tm, tn), lambda i, j: (i, j)),
        out_shape=jax.ShapeDtypeStruct((n, D_MODEL), BF16),
        compiler_params=_params("parallel", "parallel"),
        name="branch_mix",
    )(o_rw, o_gla, o_cm, p_gate, p_gate, p_gate, wb)


def _merge_network(lo, hi, r):
    step = r * 2
    if step < hi - lo:
        yield from _merge_network(lo, hi, step)
        yield from _merge_network(lo + r, hi, step)
        yield from [(i, i + r) for i in range(lo + r, hi - r, step)]
    else:
        yield (lo, lo + r)


def _sort_network(lo, hi):
    if hi - lo >= 1:
        mid = lo + (hi - lo) // 2
        yield from _sort_network(lo, mid)
        yield from _sort_network(mid + 1, hi)
        yield from _merge_network(lo, hi, 1)


def _pop_top(lists, singles, n):
    lists = list(lists)
    depth = len(lists)
    out = []
    for r in range(n):
        m = jnp.max(lists[0], axis=0, keepdims=True)
        if singles is not None:
            m = jnp.maximum(m, jnp.max(singles, axis=0, keepdims=True))
        out.append(m)
        need = n - 1 - r
        if need == 0:
            break
        hit = lists[0] == m
        for d in range(min(depth, need)):
            lists[d] = jnp.where(hit, lists[d + 1] if d + 1 < depth else NEG_INF, lists[d])
        if singles is not None:
            singles = jnp.where(singles == m, NEG_INF, singles)
    return out


def _top_rows(s, n):
    slabs = [s[SUBLANES * v:SUBLANES * (v + 1)] for v in range(s.shape[0] // SUBLANES)]
    for i, j in _sort_network(0, len(slabs) - 1):
        slabs[i], slabs[j] = jnp.maximum(slabs[i], slabs[j]), jnp.minimum(slabs[i], slabs[j])
    return _pop_top(slabs, None, n)


def _router_kernel(q_ref, keys_ref, beta_ref, thr_ref, c_ref):
    tn = q_ref.shape[0]
    n_top = PEER_TOPK + 1
    for h in range(PEER_HEADS):
        s1 = _hdot(keys_ref[2 * h], q_ref[:, LANES * (2 * h):LANES * (2 * h + 1)], _NT)
        s2 = _hdot(keys_ref[2 * h + 1], q_ref[:, LANES * (2 * h + 1):LANES * (2 * h + 2)], _NT)
        a = _top_rows(s1, n_top)
        b = _top_rows(s2, n_top)
        pad = jnp.full((3 * SUBLANES - n_top, tn), NEG_INF, F32)
        a_all = jnp.concatenate(a + [pad], axis=0)
        best = _pop_top([a_all[0:SUBLANES] + b[y] for y in range(n_top)], a_all[SUBLANES:] + b[0], n_top)
        top = best[0]
        zsum = jnp.zeros_like(top)
        for t in range(PEER_TOPK):
            zsum += jnp.exp(best[t] - top)
        tau = 0.5 * (best[PEER_TOPK - 1] + best[PEER_TOPK])
        rows = slice(SUBLANES * h, SUBLANES * (h + 1))
        beta_ref[h] = s2 - b[0]
        thr_ref[:, rows, :] = ((tau - top) - (s1 - a[0])).reshape(PEER_ITILES, SUBLANES, tn)
        c_ref[:, rows, :] = (jnp.exp(s1 - a[0]) / zsum).reshape(PEER_ITILES, SUBLANES, tn)


def peer_router(q, keys, tn):
    n = q.shape[0]
    spec = pl.BlockSpec((PEER_HEADS, PEER_NKEYS, tn), lambda i: (0, 0, i))
    shp = jax.ShapeDtypeStruct((PEER_HEADS, PEER_NKEYS, n), F32)
    spec_i = pl.BlockSpec((PEER_ITILES, PEER_HEADS * SUBLANES, tn), lambda i: (0, 0, i))
    shp_i = jax.ShapeDtypeStruct((PEER_ITILES, PEER_HEADS * SUBLANES, n), F32)
    return pl.pallas_call(
        _router_kernel,
        grid=(n // tn,),
        in_specs=[pl.BlockSpec((tn, D_MODEL), lambda i: (i, 0)),
                  pl.BlockSpec((2 * PEER_HEADS, PEER_NKEYS, PEER_DH), lambda i: (0, 0, 0))],
        out_specs=[spec, spec_i, spec_i],
        out_shape=[shp, shp_i, shp_i],
        compiler_params=_params("parallel"),
        name="peer_router",
    )(q, keys)


def _peer_kernel(xn_ref, u_ref, v_ref, beta_ref, thr_ref, c_ref, res_ref, o_ref, eb_scr, *, ti):
    e = pl.program_id(1)
    sub_i = PEER_SUB_KEYS

    @pl.when(e == 0)
    def _init():
        o_ref[...] = res_ref[...]
        eb_scr[...] = jnp.exp(beta_ref[...])

    sub = (e % (SUBLANES // ti)) * ti
    xn = xn_ref[...]
    starts = list(range(0, ti, sub_i))

    def scores(s0):
        rows = slice(s0 * PEER_NKEYS, (s0 + sub_i) * PEER_NKEYS)
        return lax.dot_general(u_ref[rows, :], xn, _NT, preferred_element_type=F32)

    def weights(s0, ht):
        gates = []
        for ii in range(s0, s0 + sub_i):
            acc = None
            for h in range(PEER_HEADS):
                r = pl.ds(SUBLANES * h + sub + ii, 1)
                hit = beta_ref[h] >= thr_ref[0, r, :]
                term = jnp.where(hit, eb_scr[h] * c_ref[0, r, :], 0.0)
                acc = term if acc is None else acc + term
            gates.append(acc)
        g = jnp.concatenate(gates, axis=0) if len(gates) > 1 else gates[0]
        return (g * _gelu(ht)).astype(BF16)

    tot = None
    ht_next = scores(starts[0])
    for n, s0 in enumerate(starts):
        ht = ht_next
        if n + 1 < len(starts):
            ht_next = scores(starts[n + 1])
        rows = slice(s0 * PEER_NKEYS, (s0 + sub_i) * PEER_NKEYS)
        part = lax.dot_general(weights(s0, ht), v_ref[rows, :], _TN, preferred_element_type=F32)
        tot = part if tot is None else tot + part
    o_ref[...] += tot


def peer_experts(xn, u, v, layer, beta, thr, cc, res, tm, ti):
    n = xn.shape[0]
    te = ti * PEER_NKEYS
    per = SUBLANES // ti
    once = dict(pipeline_mode=pl.Buffered(1))
    ispec = pl.BlockSpec((1, PEER_HEADS * SUBLANES, tm), lambda i, e: (e // per, 0, i))
    return pl.pallas_call(
        functools.partial(_peer_kernel, ti=ti),
        grid=(n // tm, PEER_NKEYS // ti),
        in_specs=[
            pl.BlockSpec((tm, D_MODEL), lambda i, e: (i, 0), **once),
            pl.BlockSpec((None, te, D_MODEL), lambda i, e: (layer, e, 0)),
            pl.BlockSpec((None, te, D_MODEL), lambda i, e: (layer, e, 0)),
            pl.BlockSpec((PEER_HEADS, PEER_NKEYS, tm), lambda i, e: (0, 0, i), **once),
            ispec, ispec,
            pl.BlockSpec((tm, D_MODEL), lambda i, e: (i, 0), **once),
        ],
        out_specs=pl.BlockSpec((tm, D_MODEL), lambda i, e: (i, 0)),
        out_shape=jax.ShapeDtypeStruct((n, D_MODEL), F32),
        scratch_shapes=[pltpu.VMEM((PEER_HEADS, PEER_NKEYS, tm), F32)],
        compiler_params=_params("parallel", "arbitrary"),
        name="peer_experts",
    )(xn, u, v, beta, thr, cc, res)


TOKEN_TILE = 640
PROMPT_CHUNKS_PER_STEP = 4
DECODE_ROWS_PER_STEP = 8
PEER_KEYS_PER_STEP = 8
W_IN_COLS = dict(rw=(0, 3328), gla_qkv=(3328, 5376), gla_xg=(5376, 5392), gla_r=(5392, 6416),
                 cm=(6416, 8464), gate=(8464, 14608))


def _proj(hn, w, tn):
    return matmul(hn, w.astype(BF16), TOKEN_TILE, tn)


def kernel(x_prompt, x_sample, state_rw_shift, state_rwkv, state_gla, norm1, w_in, rw_mu, rw_w0, rw_w2, rw_a0, rw_a2, rw_g2, rw_kk, rw_ka, rw_rk, rw_lnw, rw_lnb, gla_gw2, gla_gb, gla_lnw, gla_lnb, cm_lnw, cm_lnb, cm_ws, cm_b, w_branch, w_out, norm2, peer_wq, peer_keys, peer_u, peer_v, norm_f):
    bp, tp, d = x_prompt.shape
    bs = x_sample.shape[0]
    n_p = bp * tp
    n_tok = n_p + bs
    x = jnp.concatenate([x_prompt.reshape(n_p, d), x_sample.reshape(bs, d)], axis=0)
    head_of_lane = jnp.arange(LANES) // RW_HEAD
    ones_bd = (head_of_lane[:, None] == head_of_lane[None, :]).astype(F32)
    row = lambda a: a.reshape(1, -1)
    dec_steps = SUBLANES
    cm_chunk_p = min(CM_CHUNK, tp)
    peer_u_bf = peer_u.astype(BF16)
    peer_v_bf = peer_v.astype(BF16)
    zero_shift = jnp.zeros((1, bp, 1, RW_COLS), F32)
    zero_rw = jnp.zeros((1, bp, RW_HEADS, RW_HEAD, RW_HEAD), F32)
    zero_gla = jnp.zeros((1, bp, GLA_HEADS, GLA_HK, GLA_HV), F32)
    shift_in = state_rw_shift[:, :, None, :]
    outs = {k: [] for k in ("p_sh", "p_rw", "p_gla", "p_cm", "s_sh", "s_cm")}
    s_rw = jnp.zeros((DEPTH, bs, RW_HEADS, RW_HEAD, RW_HEAD), F32)
    s_gla = jnp.zeros((DEPTH, bs, GLA_HEADS, GLA_HK, GLA_HV), F32)
    o_rw = jnp.zeros((n_tok, BRANCH_W), F32)
    o_gla = jnp.zeros((n_tok, BRANCH_W), F32)
    o_cm = jnp.zeros((n_tok, BRANCH_W), F32)
    for l in range(DEPTH):
        prompt = lambda chunk, nc: _Group(bp, tp, chunk, 1, nc, False, 0, 0, 1)
        sample = _Group(bs, 1, dec_steps, DECODE_ROWS_PER_STEP, 1, True, n_p, l, DEPTH)
        wl = w_in[l]
        cols = lambda name: wl[:, W_IN_COLS[name][0]:W_IN_COLS[name][1]]
        hn = rmsnorm(x, norm1[l], BF16, TOKEN_TILE)
        p_rw = _proj(hn, cols("rw"), 1664)
        p_gl = _proj(hn, jnp.concatenate([cols("gla_qkv"), cols("gla_r")], axis=1), 1536)
        p_xg = _proj(hn, jnp.pad(cols("gla_xg"), ((0, 0), (0, LANES - GLA_GATE_LORA))), LANES)
        p_cm = _proj(hn, cols("cm"), 1024)
        p_gt = _proj(hn, cols("gate"), 1536)

        wa2 = jnp.zeros((LANES, 2 * BRANCH_W), F32)
        wa2 = wa2.at[0:RW_HEAD, 0:BRANCH_W].set(rw_w2[l]).at[RW_HEAD:LANES, BRANCH_W:].set(rw_a2[l])
        rw_prm = dict(mu=row(rw_mu[l]), w0=row(rw_w0[l]), wa2=wa2.astype(BF16), a0=row(rw_a0[l]),
                      g2=rw_g2[l].astype(BF16), kk=row(rw_kk[l]), ka=row(rw_ka[l]), rk=row(rw_rk[l]),
                      lnw=row(rw_lnw[l]), lnb=row(rw_lnb[l]), ones_bd=ones_bd)
        o_rw, sh_p, st_p = rwkv_branch(p_rw, zero_shift, zero_rw, rw_prm,
                                       prompt(RW_CHUNK, PROMPT_CHUNKS_PER_STEP), n_tok, o_buf=o_rw)
        o_rw, sh_s, s_rw = rwkv_branch(p_rw, shift_in, state_rwkv, rw_prm, sample, n_tok, o_buf=o_rw, s_buf=s_rw)
        outs["p_sh"].append(sh_p[:, 0])
        outs["p_rw"].append(st_p[0])
        outs["s_sh"].append(sh_s[:, 0])

        gla_prm = dict(gw2=jnp.pad(gla_gw2[l], ((0, LANES - GLA_GATE_LORA), (0, 0))).astype(BF16),
                       gb=row(gla_gb[l]), lnw=row(gla_lnw[l]), lnb=row(gla_lnb[l]))
        o_gla, st_p = gla_branch(p_gl, p_xg, zero_gla, gla_prm, prompt(GLA_CHUNK, PROMPT_CHUNKS_PER_STEP), n_tok,
                                 o_buf=o_gla)
        o_gla, s_gla = gla_branch(p_gl, p_xg, state_gla, gla_prm, sample, n_tok, o_buf=o_gla, s_buf=s_gla)
        outs["p_gla"].append(st_p[0])

        cm_prm = lambda c: dict(lnw=row(cm_lnw[l]), lnb=row(cm_lnb[l]), ws=cm_ws[l][:, :c, :c],
                                bt=jnp.transpose(cm_b[l])[:c])
        o_cm, v_p = cmlp_branch(p_cm, cm_prm(cm_chunk_p), prompt(cm_chunk_p, 1), n_tok, o_buf=o_cm)
        o_cm, v_s = cmlp_branch(p_cm, cm_prm(dec_steps), sample, n_tok, o_buf=o_cm)
        outs["p_cm"].append(v_p[:, :tp - ((tp - 1) // CM_CHUNK) * CM_CHUNK])
        outs["s_cm"].append(v_s[:, 0:1])

        mix = branch_mix(o_rw, o_gla, o_cm, p_gt, w_branch[l].astype(BF16), TOKEN_TILE, 1024)
        x = matmul(mix, w_out[l].astype(BF16), TOKEN_TILE, 1024, res=x)

        hf = rmsnorm(x, norm2[l], BF16, TOKEN_TILE)
        q = matmul(hf, peer_wq[l].astype(BF16), TOKEN_TILE, 1024)
        beta, thr, cc = peer_router(q, peer_keys[l].reshape(2 * PEER_HEADS, PEER_NKEYS, PEER_DH), TOKEN_TILE)
        x = peer_experts(hf, peer_u_bf, peer_v_bf, l, beta, thr, cc, x, TOKEN_TILE, PEER_KEYS_PER_STEP)

    y = rmsnorm(x, norm_f, F32, TOKEN_TILE)
    st = lambda k: jnp.stack(outs[k])
    return (y[:n_p].reshape(bp, tp, d), y[n_p:].reshape(bs, 1, d), st("p_sh"), st("p_rw"), st("p_gla"), st("p_cm"),
            st("s_sh"), s_rw, s_gla, st("s_cm"))
```
